```python
import jax, jax.numpy as jnp
from jax import lax
import numpy as np

D_MODEL = 1024
BATCH = 32
SEQ = 2048
DEPTH = 1
DEC_BATCH = 2
DEC_SEQ = 16384
PAST_LEN = 128

N_Q_HEADS = 8
N_KV_HEADS = 2
HEAD_DIM = 64
ATTN_WIDTH = N_Q_HEADS * HEAD_DIM
KV_WIDTH = N_KV_HEADS * HEAD_DIM
WINDOW = 128
BLOCK = 128
ROPE_THETA = 10000.0
LRU_WIDTH = 512
LRU_BLOCKS = 8
LRU_BLOCK_DIM = LRU_WIDTH // LRU_BLOCKS
LRU_C = 8.0
CONV_WIDTH = 4
CONV_LEFT = 2
MIX_WIDTH = ATTN_WIDTH + LRU_WIDTH
IN_WIDTH = ATTN_WIDTH + 2 * KV_WIDTH + 2 * LRU_WIDTH
PEER_HEADS = 8
PEER_NKEYS = 128
PEER_EXPERTS = PEER_NKEYS * PEER_NKEYS
PEER_DKEY = 256
PEER_HALF = PEER_DKEY // 2
PEER_TOPK = 16
PEER_CHUNK = 128
PLE_DIM = 256
EPS = 1e-6

kernel_name = 'hymba_swa_rglru_peer_encoder'


def rms_norm(x, g):
    xf = x.astype(jnp.float32)
    y = xf * lax.rsqrt(jnp.mean(xf * xf, axis=-1, keepdims=True) + EPS)
    return (y * g.astype(jnp.float32)).astype(x.dtype)


def apply_rope(t):
    seq = t.shape[1]
    half = HEAD_DIM // 2
    inv_freq = ROPE_THETA ** (-jnp.arange(half, dtype=jnp.float32) / half)
    ang = jnp.arange(seq, dtype=jnp.float32)[:, None] * inv_freq[None, :]
    cos = jnp.cos(ang)[None, :, None, :]
    sin = jnp.sin(ang)[None, :, None, :]
    tf = t.astype(jnp.float32)
    t1, t2 = tf[..., :half], tf[..., half:]
    return jnp.concatenate([t1 * cos - t2 * sin, t2 * cos + t1 * sin], axis=-1).astype(t.dtype)


def banded_window_attention(q, k, v, sink):
    b, s = q.shape[0], q.shape[1]
    nb = s // BLOCK
    grp = N_Q_HEADS // N_KV_HEADS
    qb = q.astype(jnp.float32).reshape(b, nb, BLOCK, N_KV_HEADS, grp, HEAD_DIM)

    def bands(t):
        tp = jnp.pad(t.astype(jnp.float32), ((0, 0), (BLOCK, BLOCK), (0, 0), (0, 0)))
        tp = tp.reshape(b, nb + 2, BLOCK, N_KV_HEADS, HEAD_DIM)
        return jnp.concatenate([tp[:, :-2], tp[:, 1:-1], tp[:, 2:]], axis=2)

    kb = bands(k)
    vb = bands(v)
    scores = jnp.einsum('bnqkgd,bnckd->bnkgqc', qb, kb) * (HEAD_DIM ** -0.5)
    qpos = jnp.arange(nb)[:, None] * BLOCK + jnp.arange(BLOCK)[None, :]
    kpos = (jnp.arange(nb)[:, None] - 1) * BLOCK + jnp.arange(3 * BLOCK)[None, :]
    valid = ((jnp.abs(qpos[:, :, None] - kpos[:, None, :]) <= WINDOW)
             & (kpos[:, None, :] >= 0) & (kpos[:, None, :] < s))
    scores = jnp.where(valid[None, :, None, None], scores, -jnp.inf)
    sink_l = sink.astype(jnp.float32).reshape(N_KV_HEADS, grp)[None, None, :, :, None, None]
    m = jnp.maximum(jnp.max(scores, axis=-1, keepdims=True), sink_l)
    e = jnp.exp(scores - m)
    denom = jnp.sum(e, axis=-1, keepdims=True) + jnp.exp(sink_l - m)
    out = jnp.einsum('bnkgqc,bnckd->bnqkgd', e / denom, vb)
    return out.reshape(b, s, ATTN_WIDTH)


def centred_dwconv(x, w, bias):
    y = lax.conv_general_dilated(
        x, w[:, None, :].astype(x.dtype), window_strides=(1,),
        padding=[(CONV_LEFT, CONV_WIDTH - 1 - CONV_LEFT)],
        dimension_numbers=('NWC', 'WIO', 'NWC'), feature_group_count=LRU_WIDTH)
    return y + bias.astype(x.dtype)


def linear_combine(left, right):
    a_l, h_l = left
    a_r, h_r = right
    return a_l * a_r, a_r * h_l + h_r


def rglru_scan(xc, wa, ba, wx, bx, lam):
    b, s, _ = xc.shape
    xf = xc.astype(jnp.float32)
    xb = xf.reshape(b, s, LRU_BLOCKS, LRU_BLOCK_DIM)
    gate_r = jax.nn.sigmoid(jnp.einsum('bshi,hij->bshj', xb, wa.astype(jnp.float32)).reshape(b, s, LRU_WIDTH)
                            + ba.astype(jnp.float32))
    gate_i = jax.nn.sigmoid(jnp.einsum('bshi,hij->bshj', xb, wx.astype(jnp.float32)).reshape(b, s, LRU_WIDTH)
                            + bx.astype(jnp.float32))
    log_a = LRU_C * gate_r * jax.nn.log_sigmoid(lam.astype(jnp.float32))
    a = jnp.exp(log_a)
    u = jnp.sqrt(-jnp.expm1(2.0 * log_a)) * (gate_i * xf)
    _, h = lax.associative_scan(linear_combine, (a, u), axis=1)
    return h


def peer_ffn(x, wq, keys, u_tab, v_tab):
    b, s, d = x.shape
    xt = x.reshape(-1, PEER_CHUNK, d)

    def retrieve(xc):
        q = (xc @ wq).astype(jnp.float32).reshape(PEER_CHUNK, PEER_HEADS, 2, PEER_HALF)
        sub = jnp.einsum('thpd,hpkd->thpk', q, keys.astype(jnp.float32))
        sub_s, sub_i = lax.top_k(sub, PEER_TOPK)
        cand_s = (sub_s[:, :, 0, :, None] + sub_s[:, :, 1, None, :]).reshape(PEER_CHUNK, PEER_HEADS, -1)
        cand_i = (sub_i[:, :, 0, :, None] * PEER_NKEYS + sub_i[:, :, 1, None, :]).reshape(PEER_CHUNK, PEER_HEADS, -1)
        best_s, pos = lax.top_k(cand_s, PEER_TOPK)
        idx = jnp.take_along_axis(cand_i, pos, axis=-1)
        g = jax.nn.softmax(best_s, axis=-1)
        u_e = jnp.take(u_tab, idx, axis=0).astype(jnp.float32)
        act = jax.nn.gelu(jnp.einsum('td,thkd->thk', xc.astype(jnp.float32), u_e))
        v_e = jnp.take(v_tab, idx, axis=0).astype(jnp.float32)
        return jnp.einsum('thk,thkd->td', g * act, v_e).astype(xc.dtype)

    return lax.map(retrieve, xt).reshape(b, s, d)


def setup_inputs(seed: int = 0) -> dict:
    key = jax.random.key(seed)
    ks = jax.random.split(key, 26)
    nrm = jax.random.normal
    f32 = jnp.float32
    uu = jax.random.uniform(ks[13], (DEPTH, 2, LRU_WIDTH), f32, minval=0.9, maxval=0.999)
    a0 = uu ** (1.0 / LRU_C)
    lru_lambda = jnp.log(a0) - jnp.log1p(-a0)
    return {
        'x_prompt': nrm(ks[0], (BATCH, SEQ, D_MODEL), f32),
        'x_sample': nrm(ks[1], (DEC_BATCH, DEC_SEQ, D_MODEL), f32),
        'p_prompt': nrm(ks[2], (DEPTH, BATCH, SEQ, PLE_DIM), f32),
        'p_sample': nrm(ks[3], (DEPTH, DEC_BATCH, DEC_SEQ, PLE_DIM), f32),
        'mix_norm_g': 1.0 + 0.05 * nrm(ks[4], (DEPTH, D_MODEL), f32),
        'w_in': nrm(ks[5], (DEPTH, D_MODEL, IN_WIDTH), f32) * D_MODEL ** -0.5,
        'attn_sink': 0.5 * nrm(ks[6], (DEPTH, N_Q_HEADS), f32),
        'conv_w': 0.5 * nrm(ks[7], (DEPTH, CONV_WIDTH, LRU_WIDTH), f32),
        'conv_b': 0.01 * nrm(ks[8], (DEPTH, LRU_WIDTH), f32),
        'lru_wa': nrm(ks[9], (DEPTH, 2, LRU_BLOCKS, LRU_BLOCK_DIM, LRU_BLOCK_DIM), f32) * LRU_BLOCK_DIM ** -0.5,
        'lru_ba': 0.01 * nrm(ks[10], (DEPTH, 2, LRU_WIDTH), f32),
        'lru_wx': nrm(ks[11], (DEPTH, 2, LRU_BLOCKS, LRU_BLOCK_DIM, LRU_BLOCK_DIM), f32) * LRU_BLOCK_DIM ** -0.5,
        'lru_bx': 0.01 * nrm(ks[12], (DEPTH, 2, LRU_WIDTH), f32),
        'lru_lambda': lru_lambda,
        'attn_out_norm_g': 1.0 + 0.05 * nrm(ks[14], (DEPTH, ATTN_WIDTH), f32),
        'lru_out_norm_g': 1.0 + 0.05 * nrm(ks[15], (DEPTH, LRU_WIDTH), f32),
        'w_out': nrm(ks[16], (DEPTH, MIX_WIDTH, D_MODEL), f32) * MIX_WIDTH ** -0.5,
        'ffn_norm_g': 1.0 + 0.05 * nrm(ks[17], (DEPTH, D_MODEL), f32),
        'peer_wq': nrm(ks[18], (DEPTH, D_MODEL, PEER_HEADS * PEER_DKEY), f32) * D_MODEL ** -0.5,
        'peer_keys': nrm(ks[19], (DEPTH, PEER_HEADS, 2, PEER_NKEYS, PEER_HALF), f32) * PEER_HALF ** -0.5,
        'peer_u': nrm(ks[20], (DEPTH, PEER_EXPERTS, D_MODEL), f32) * D_MODEL ** -0.5,
        'peer_v': 0.3 * nrm(ks[21], (DEPTH, PEER_EXPERTS, D_MODEL), f32),
        'ple_norm_g': 1.0 + 0.05 * nrm(ks[22], (DEPTH, D_MODEL), f32),
        'ple_w_gate': nrm(ks[23], (DEPTH, D_MODEL, D_MODEL), f32) * D_MODEL ** -0.5,
        'ple_w_proj': nrm(ks[24], (DEPTH, PLE_DIM, D_MODEL), f32) * PLE_DIM ** -0.5,
        'final_norm_g': 1.0 + 0.05 * nrm(ks[25], (D_MODEL,), f32),
    }


def reference(x_prompt, x_sample, p_prompt, p_sample, mix_norm_g, w_in, attn_sink, conv_w, conv_b,
              lru_wa, lru_ba, lru_wx, lru_bx, lru_lambda, attn_out_norm_g, lru_out_norm_g, w_out,
              ffn_norm_g, peer_wq, peer_keys, peer_u, peer_v, ple_norm_g, ple_w_gate, ple_w_proj,
              final_norm_g):
    o1 = ATTN_WIDTH
    o2 = o1 + KV_WIDTH
    o3 = o2 + KV_WIDTH
    o4 = o3 + LRU_WIDTH

    def encoder(x, p):
        h = x
        b, s, _ = x.shape
        for l in range(DEPTH):
            xn = rms_norm(h, mix_norm_g[l])
            z = xn @ w_in[l]
            q, k, v, xr, gr = jnp.split(z, [o1, o2, o3, o4], axis=-1)
            q = apply_rope(q.reshape(b, s, N_Q_HEADS, HEAD_DIM))
            k = apply_rope(k.reshape(b, s, N_KV_HEADS, HEAD_DIM))
            v = v.reshape(b, s, N_KV_HEADS, HEAD_DIM)
            attn = banded_window_attention(q, k, v, attn_sink[l]).astype(h.dtype)
            xc = centred_dwconv(xr, conv_w[l], conv_b[l])
            h_fwd = rglru_scan(xc, lru_wa[l, 0], lru_ba[l, 0], lru_wx[l, 0], lru_bx[l, 0], lru_lambda[l, 0])
            h_bwd = jnp.flip(rglru_scan(jnp.flip(xc, axis=1), lru_wa[l, 1], lru_ba[l, 1], lru_wx[l, 1],
                                        lru_bx[l, 1], lru_lambda[l, 1]), axis=1)
            lru = (jax.nn.gelu(gr.astype(jnp.float32)) * (h_fwd + h_bwd)).astype(h.dtype)
            merged = jnp.concatenate([rms_norm(attn, attn_out_norm_g[l]), rms_norm(lru, lru_out_norm_g[l])], axis=-1)
            h = h + merged @ w_out[l]
            h = h + peer_ffn(rms_norm(h, ffn_norm_g[l]), peer_wq[l], peer_keys[l], peer_u[l], peer_v[l])
            gate = jax.nn.sigmoid(rms_norm(h, ple_norm_g[l]) @ ple_w_gate[l])
            h = h + gate * (p[l] @ ple_w_proj[l])
        return rms_norm(h, final_norm_g)

    y_prompt = encoder(x_prompt, p_prompt)
    y_sample = encoder(x_sample, p_sample)
    return (y_prompt, y_sample)
```

```python
import functools

import jax
import jax.numpy as jnp
import numpy as np
from jax import lax
from jax.experimental import pallas as pl
from jax.experimental.pallas import tpu as pltpu

F32 = jnp.float32
BF16 = jnp.bfloat16

EPS = 1e-6
N_Q_HEADS = 8
N_KV_HEADS = 2
HEAD_DIM = 64
WINDOW = 128
ROPE_THETA = 10000.0
LRU_C = 8.0
CONV_LEFT = 2
PEER_TOPK = 16

LANE = 128
SUBLANE = 8
ROW_TILE = 512
PEER_TOKENS = 512
PEER_SUB = SUBLANE * LANE
PEER_SUBS_PER_STEP = 2
VMEM_LIMIT = 56 * 1024 * 1024
NEG_INF = float("-inf")


def _cparams(*sem):
    return pltpu.CompilerParams(dimension_semantics=sem, vmem_limit_bytes=VMEM_LIMIT)


def _rms(x, g):
    return x * lax.rsqrt(jnp.mean(x * x, axis=-1, keepdims=True) + EPS) * g


def _in_proj_kernel(x_ref, g_ref, w_ref, cos_ref, sin_ref, q_ref, k_ref, v_ref, xr_ref, gr_ref):
    aw, kw, lw = q_ref.shape[1], LANE, xr_ref.shape[1]
    xn = _rms(x_ref[...], g_ref[...]).astype(BF16)
    z = jnp.dot(xn, w_ref[...], preferred_element_type=F32)
    c = cos_ref[...]
    s = sin_ref[...]
    reps = aw // LANE
    cq = jnp.concatenate([c] * reps, axis=1)
    sq = jnp.concatenate([s] * reps, axis=1)
    o = 0
    q = (z[:, o:o + aw] * cq + z[:, o + aw:o + 2 * aw] * sq) * (HEAD_DIM ** -0.5)
    o += 2 * aw
    k = z[:, o:o + kw] * c + z[:, o + kw:o + 2 * kw] * s
    o += 2 * kw
    v = z[:, o:o + kw]
    o += kw
    q_ref[...] = q.astype(BF16)
    k_ref[...] = jnp.concatenate([k, pltpu.roll(k, HEAD_DIM, 1)], axis=1).astype(BF16)
    v_ref[...] = jnp.concatenate([v, pltpu.roll(v, HEAD_DIM, 1)], axis=1).astype(BF16)
    xr_ref[...] = z[:, o:o + lw]
    gr_ref[...] = z[:, o + lw:o + 2 * lw]


def _in_proj(x, g, w_ext, cos, sin, seq, aw, lw):
    n, d = x.shape
    tm = ROW_TILE
    per_seq = seq // tm
    row = lambda i: (i, 0)
    const = lambda i: (0, 0)
    pos = lambda i: (i % per_seq, 0)
    return pl.pallas_call(
        _in_proj_kernel,
        grid=(n // tm,),
        in_specs=[pl.BlockSpec((tm, d), row), pl.BlockSpec((1, d), const),
                  pl.BlockSpec(w_ext.shape, const),
                  pl.BlockSpec((tm, LANE), pos), pl.BlockSpec((tm, LANE), pos)],
        out_specs=[pl.BlockSpec((tm, aw), row), pl.BlockSpec((tm, 2 * LANE), row),
                   pl.BlockSpec((tm, 2 * LANE), row), pl.BlockSpec((tm, lw), row),
                   pl.BlockSpec((tm, lw), row)],
        out_shape=[jax.ShapeDtypeStruct((n, aw), BF16), jax.ShapeDtypeStruct((n, 2 * LANE), BF16),
                   jax.ShapeDtypeStruct((n, 2 * LANE), BF16), jax.ShapeDtypeStruct((n, lw), F32),
                   jax.ShapeDtypeStruct((n, lw), F32)],
        compiler_params=_cparams("parallel"),
        name="in_proj",
    )(x, g, w_ext, cos, sin)


def _attn_kernel(sink_ref, q_ref, kp_ref, ks_ref, kn_ref, vp_ref, vs_ref, vn_ref, g_ref, o_ref, *, seq):
    qb = q_ref.shape[0]
    blk = WINDOW
    p0 = (pl.program_id(0) % (seq // qb)) * qb
    kf = jnp.concatenate([kp_ref[...], ks_ref[...], kn_ref[...]], axis=0)
    vf = jnp.concatenate([vp_ref[...], vs_ref[...], vn_ref[...]], axis=0)
    lane = lax.broadcasted_iota(jnp.int32, (blk, LANE), 1)
    row = lax.broadcasted_iota(jnp.int32, (blk, 3 * blk), 0)
    col = lax.broadcasted_iota(jnp.int32, (blk, 3 * blk), 1)
    rel = col - row
    band = (rel >= 0) & (rel <= 2 * WINDOW)
    grp = N_Q_HEADS // N_KV_HEADS
    for r in range(qb // blk):
        kpos = p0 + (r - 1) * blk + col
        valid = band & (kpos >= 0) & (kpos < seq)
        pairs = []
        for m in range(N_Q_HEADS // 2):
            qs = q_ref[r * blk:(r + 1) * blk, m * LANE:(m + 1) * LANE]
            halves = []
            for half in range(2):
                h = 2 * m + half
                kvh = h // grp
                sel = 0 if half == kvh else 1
                qm = jnp.where((lane >= HEAD_DIM) == (half == 1), qs, jnp.zeros_like(qs))
                kk = kf[r * blk:(r + 3) * blk, sel * LANE:(sel + 1) * LANE]
                vv = vf[r * blk:(r + 3) * blk, sel * LANE:(sel + 1) * LANE]
                s = lax.dot_general(qm, kk, (((1,), (1,)), ((), ())), preferred_element_type=F32)
                s = jnp.where(valid, s, -1e30)
                sk = sink_ref[h]
                mx = jnp.maximum(jnp.max(s, axis=1, keepdims=True), sk)
                e = jnp.exp(s - mx)
                den = jnp.sum(e, axis=1, keepdims=True) + jnp.exp(sk - mx)
                p = (e / den).astype(BF16)
                halves.append(jnp.dot(p, vv, preferred_element_type=F32))
            pairs.append(jnp.where(lane < HEAD_DIM, halves[0], halves[1]))
        attn = jnp.concatenate(pairs, axis=1)
        o_ref[r * blk:(r + 1) * blk, :] = _rms(attn, g_ref[...]).astype(BF16)


def _attention(q, k2, v2, sink, g, seq):
    n, aw = q.shape
    qb = ROW_TILE
    per = qb // WINDOW
    nblk = n // WINDOW
    row = lambda i: (i, 0)
    prev = lambda i: (jnp.maximum(i * per - 1, 0), 0)
    nxt = lambda i: (jnp.minimum((i + 1) * per, nblk - 1), 0)
    kw = k2.shape[1]
    return pl.pallas_call(
        functools.partial(_attn_kernel, seq=seq),
        grid=(n // qb,),
        in_specs=[pl.BlockSpec(memory_space=pltpu.SMEM),
                  pl.BlockSpec((qb, aw), row),
                  pl.BlockSpec((WINDOW, kw), prev), pl.BlockSpec((qb, kw), row), pl.BlockSpec((WINDOW, kw), nxt),
                  pl.BlockSpec((WINDOW, kw), prev), pl.BlockSpec((qb, kw), row), pl.BlockSpec((WINDOW, kw), nxt),
                  pl.BlockSpec((1, aw), lambda i: (0, 0))],
        out_specs=pl.BlockSpec((qb, aw), row),
        out_shape=jax.ShapeDtypeStruct((n, aw), BF16),
        compiler_params=_cparams("parallel"),
        name="attention",
    )(sink, q, k2, k2, k2, v2, v2, v2, g)


def _lru_gates(x_ref, xp_ref, xn_ref, cw_ref, cb_ref, wg_ref, bg_ref, lam_ref, ext_ref, a_ref, u_ref,
               has_prev, has_next):
    tb, lw = x_ref.shape
    x = x_ref[...]
    ext_ref[0:SUBLANE, :] = jnp.where(has_prev, xp_ref[...], 0.0)
    ext_ref[SUBLANE:SUBLANE + tb, :] = x
    ext_ref[SUBLANE + tb:2 * SUBLANE + tb, :] = jnp.where(has_next, xn_ref[...], 0.0)
    cw = cw_ref[...]
    xc = cb_ref[...] + cw[2:3] * x
    for j in (0, 1, 3):
        off = SUBLANE + j - CONV_LEFT
        xc = xc + cw[j:j + 1] * ext_ref[off:off + tb, :]
    gates = jnp.dot(xc.astype(BF16), wg_ref[...], preferred_element_type=F32) + bg_ref[...]
    gate_r = jax.nn.sigmoid(gates[:, :lw])
    gate_i = jax.nn.sigmoid(gates[:, lw:])
    lam = lam_ref[...]
    log_sig = jnp.minimum(lam, 0.0) - jnp.log1p(jnp.exp(-jnp.abs(lam)))
    log_a = LRU_C * gate_r * log_sig
    a = jnp.exp(log_a)
    a_ref[...] = a
    u_ref[...] = jnp.sqrt(-jnp.tanh(log_a) * (a * a + 1.0)) * (gate_i * xc)


def _lru_fwd_kernel(x_ref, xp_ref, xn_ref, cw_ref, cb_ref, wg_ref, bg_ref, lam_ref, h_ref,
                    ext_ref, a_ref, u_ref, carry_ref):
    j = pl.program_id(1)
    nt = pl.num_programs(1)
    tb = x_ref.shape[0]

    @pl.when(j == 0)
    def _():
        carry_ref[...] = jnp.zeros_like(carry_ref)

    _lru_gates(x_ref, xp_ref, xn_ref, cw_ref, cb_ref, wg_ref, bg_ref, lam_ref, ext_ref, a_ref, u_ref,
               j > 0, j < nt - 1)

    def body(t, h):
        h = a_ref[pl.ds(t, 1), :] * h + u_ref[pl.ds(t, 1), :]
        h_ref[pl.ds(t, 1), :] = h
        return h

    carry_ref[...] = lax.fori_loop(0, tb, body, carry_ref[...], unroll=8)


def _lru_bwd_kernel(x_ref, xp_ref, xn_ref, cw_ref, cb_ref, wg_ref, bg_ref, lam_ref, gr_ref, hf_ref, g_ref,
                    o_ref, ext_ref, a_ref, u_ref, hb_ref, carry_ref):
    j = pl.program_id(1)
    nt = pl.num_programs(1)
    tb = x_ref.shape[0]

    @pl.when(j == 0)
    def _():
        carry_ref[...] = jnp.zeros_like(carry_ref)

    _lru_gates(x_ref, xp_ref, xn_ref, cw_ref, cb_ref, wg_ref, bg_ref, lam_ref, ext_ref, a_ref, u_ref,
               j < nt - 1, j > 0)

    def body(i, h):
        t = tb - 1 - i
        h = a_ref[pl.ds(t, 1), :] * h + u_ref[pl.ds(t, 1), :]
        hb_ref[pl.ds(t, 1), :] = h
        return h

    carry_ref[...] = lax.fori_loop(0, tb, body, carry_ref[...], unroll=8)
    lru = jax.nn.gelu(gr_ref[...]) * (hf_ref[...] + hb_ref[...])
    o_ref[...] = _rms(lru, g_ref[...]).astype(BF16)


def _lru_specs(batch, seq, lw, tb, reverse):
    nt = seq // tb
    per8 = tb // SUBLANE
    n8 = batch * seq // SUBLANE

    def blk(b, j):
        return b * nt + (nt - 1 - j if reverse else j)

    main = lambda b, j: (blk(b, j), 0)
    prev = lambda b, j: (jnp.maximum(blk(b, j) * per8 - 1, 0), 0)
    nxt = lambda b, j: (jnp.minimum((blk(b, j) + 1) * per8, n8 - 1), 0)
    const = lambda b, j: (0, 0)
    specs = [pl.BlockSpec((tb, lw), main), pl.BlockSpec((SUBLANE, lw), prev), pl.BlockSpec((SUBLANE, lw), nxt),
             pl.BlockSpec((4, lw), const), pl.BlockSpec((1, lw), const),
             pl.BlockSpec((lw, 2 * lw), const), pl.BlockSpec((1, 2 * lw), const), pl.BlockSpec((1, lw), const)]
    return specs, main, const, nt


def _lru_fwd(xr, cw, cb, wg, bg, lam, batch, seq):
    n, lw = xr.shape
    tb = ROW_TILE
    specs, main, _, nt = _lru_specs(batch, seq, lw, tb, reverse=False)
    return pl.pallas_call(
        _lru_fwd_kernel,
        grid=(batch, nt),
        in_specs=specs,
        out_specs=pl.BlockSpec((tb, lw), main),
        out_shape=jax.ShapeDtypeStruct((n, lw), F32),
        scratch_shapes=[pltpu.VMEM((tb + 2 * SUBLANE, lw), F32), pltpu.VMEM((tb, lw), F32),
                        pltpu.VMEM((tb, lw), F32), pltpu.VMEM((1, lw), F32)],
        compiler_params=_cparams("parallel", "arbitrary"),
        name="lru_fwd",
    )(xr, xr, xr, cw, cb, wg, bg, lam)


def _lru_bwd(xr, cw, cb, wg, bg, lam, gr, hf, g, batch, seq):
    n, lw = xr.shape
    tb = ROW_TILE
    specs, main, const, nt = _lru_specs(batch, seq, lw, tb, reverse=True)
    specs = specs + [pl.BlockSpec((tb, lw), main), pl.BlockSpec((tb, lw), main), pl.BlockSpec((1, lw), const)]
    return pl.pallas_call(
        _lru_bwd_kernel,
        grid=(batch, nt),
        in_specs=specs,
        out_specs=pl.BlockSpec((tb, lw), main),
        out_shape=jax.ShapeDtypeStruct((n, lw), BF16),
        scratch_shapes=[pltpu.VMEM((tb + 2 * SUBLANE, lw), F32), pltpu.VMEM((tb, lw), F32),
                        pltpu.VMEM((tb, lw), F32), pltpu.VMEM((tb, lw), F32), pltpu.VMEM((1, lw), F32)],
        compiler_params=_cparams("parallel", "arbitrary"),
        name="lru_bwd",
    )(xr, xr, xr, cw, cb, wg, bg, lam, gr, hf, g)


def _out_proj_kernel(x_ref, at_ref, lr_ref, wo_ref, g_ref, wqt_ref, h1_ref, xnt_ref, qt_ref):
    aw = at_ref.shape[1]
    h1 = (x_ref[...] + jnp.dot(at_ref[...], wo_ref[0:aw, :], preferred_element_type=F32)
          + jnp.dot(lr_ref[...], wo_ref[aw:, :], preferred_element_type=F32))
    h1_ref[...] = h1
    xnt = _rms(h1, g_ref[...]).T.astype(BF16)
    xnt_ref[0] = xnt
    qt_ref[0] = jnp.dot(wqt_ref[...], xnt, preferred_element_type=F32).astype(BF16)


def _out_proj(x, attn_n, lru_n, w_out, g, wq_t):
    n, d = x.shape
    tm = PEER_TOKENS
    aw, lw = attn_n.shape[1], lru_n.shape[1]
    qd = wq_t.shape[0]
    row = lambda i: (i, 0)
    const = lambda i: (0, 0)
    tile = lambda i: (i, 0, 0)
    return pl.pallas_call(
        _out_proj_kernel,
        grid=(n // tm,),
        in_specs=[pl.BlockSpec((tm, d), row), pl.BlockSpec((tm, aw), row), pl.BlockSpec((tm, lw), row),
                  pl.BlockSpec(w_out.shape, const), pl.BlockSpec((1, d), const), pl.BlockSpec(wq_t.shape, const)],
        out_specs=[pl.BlockSpec((tm, d), row), pl.BlockSpec((1, d, tm), tile), pl.BlockSpec((1, qd, tm), tile)],
        out_shape=[jax.ShapeDtypeStruct((n, d), F32), jax.ShapeDtypeStruct((n // tm, d, tm), BF16),
                   jax.ShapeDtypeStruct((n // tm, qd, tm), BF16)],
        compiler_params=_cparams("parallel"),
        name="out_proj",
    )(x, attn_n, lru_n, w_out, g, wq_t)


def _peer_select(qt_ref, keys_ref, s_ref, e_ref, a_ref, tau_ref):
    nhp = keys_ref.shape[0]
    nk = keys_ref.shape[1]
    t = qt_ref.shape[2]
    nch = t // LANE
    dk = keys_ref.shape[2]

    def scores(hp, _):
        q = qt_ref[0, pl.ds(pl.multiple_of(hp * dk, dk), dk), :]
        s_ref[hp] = jnp.dot(keys_ref[hp], q, preferred_element_type=F32)
        return 0

    lax.fori_loop(0, nhp, scores, 0)

    def topk(hp, _):
        for c in range(nch):
            cs = slice(c * LANE, (c + 1) * LANE)
            s0 = s_ref[hp, :, cs]
            s = s0
            for k in range(PEER_TOPK):
                m = jnp.max(s, axis=0, keepdims=True)
                a_ref[hp, k:k + 1, cs] = m
                if k == 0:
                    e_ref[hp, :, cs] = jnp.exp(s0 - m)
                s = jnp.where(s == m, NEG_INF, s)
        return 0

    lax.fori_loop(0, nhp, topk, 0)

    rows8 = lax.broadcasted_iota(jnp.int32, (SUBLANE, LANE), 0)

    def cands(h, _):
        for c in range(nch):
            cs = slice(c * LANE, (c + 1) * LANE)
            a = a_ref[2 * h, :, cs]
            b = a_ref[2 * h + 1, :, cs]
            a_lo, b_lo = a[0:SUBLANE], b[0:SUBLANE]
            blocks = [a[0:1] + b_lo, a[0:1] + b[SUBLANE:], a[SUBLANE:] + b[0:1]]
            blocks += [a[i:i + 1] + b_lo for i in (1, 2, 3)]
            blocks += [jnp.where(rows8 >= 4, a_lo + b[j:j + 1], NEG_INF) for j in (0, 1, 2)]
            top = a[0:1] + b[0:1]
            z = jnp.zeros((1, LANE), F32)
            m = top
            for k in range(PEER_TOPK):
                m = blocks[0]
                for blk in blocks[1:]:
                    m = jnp.maximum(m, blk)
                m = jnp.max(m, axis=0, keepdims=True)
                z = z + jnp.exp(m - top)
                if k + 1 < PEER_TOPK:
                    blocks = [jnp.where(blk == m, NEG_INF, blk) for blk in blocks]
            tau_ref[h, :, cs] = jnp.broadcast_to(m, (SUBLANE, LANE))
            e_ref[2 * h, :, cs] = e_ref[2 * h, :, cs] * (1.0 / z)
        return 0

    lax.fori_loop(0, nhp // 2, cands, 0)


def _peer_kernel(h1_ref, xnt_ref, qt_ref, keys_ref, u_ref, vt_ref, o_ref,
                 s_ref, e_ref, a_ref, tau_ref, p_ref, acc_ref):
    eb = pl.program_id(1)
    nsub = u_ref.shape[0]
    sub = u_ref.shape[1]
    nk = keys_ref.shape[1]
    nh = keys_ref.shape[0] // 2
    t = xnt_ref.shape[2]
    nch = t // LANE
    assert sub == SUBLANE * nk

    @pl.when(eb == 0)
    def _():
        _peer_select(qt_ref, keys_ref, s_ref, e_ref, a_ref, tau_ref)
        acc_ref[...] = jnp.zeros_like(acc_ref)

    def sub_block(sb, _):
        act = jax.nn.gelu(jnp.dot(u_ref[sb], xnt_ref[0], preferred_element_type=F32))
        i0 = pl.multiple_of((eb * nsub + sb) * SUBLANE, SUBLANE)
        for ii in range(SUBLANE):
            for c in range(nch):
                cs = slice(c * LANE, (c + 1) * LANE)
                w = jnp.zeros((nk, LANE), F32)
                for h in range(nh):
                    s1 = s_ref[2 * h, pl.ds(i0, SUBLANE), cs][ii:ii + 1]
                    e1 = e_ref[2 * h, pl.ds(i0, SUBLANE), cs][ii:ii + 1]
                    hit = (s_ref[2 * h + 1, :, cs] + s1) >= tau_ref[h, 0:1, cs]
                    w = w + jnp.where(hit, e_ref[2 * h + 1, :, cs] * e1, 0.0)
                p_ref[ii * nk:(ii + 1) * nk, cs] = (w * act[ii * nk:(ii + 1) * nk, cs]).astype(BF16)
        acc_ref[...] += jnp.dot(vt_ref[sb], p_ref[...], preferred_element_type=F32)
        return 0

    lax.fori_loop(0, nsub, sub_block, 0)

    @pl.when(eb == pl.num_programs(1) - 1)
    def _():
        o_ref[...] = h1_ref[...] + acc_ref[...].T


def _peer(h1, xn_t, q_t, keys, u3, vt3):
    n, d = h1.shape
    nt, _, t = xn_t.shape
    nblk, sub, _ = u3.shape
    nsub = PEER_SUBS_PER_STEP
    nhp, nk, dk = keys.shape
    tile = lambda i, e: (i, 0, 0)
    return pl.pallas_call(
        _peer_kernel,
        grid=(nt, nblk // nsub),
        in_specs=[pl.BlockSpec((t, d), lambda i, e: (i, 0)),
                  pl.BlockSpec((1, d, t), tile), pl.BlockSpec((1, q_t.shape[1], t), tile),
                  pl.BlockSpec(keys.shape, lambda i, e: (0, 0, 0)),
                  pl.BlockSpec((nsub, sub, d), lambda i, e: (e, 0, 0)),
                  pl.BlockSpec((nsub, d, sub), lambda i, e: (e, 0, 0))],
        out_specs=pl.BlockSpec((t, d), lambda i, e: (i, 0)),
        out_shape=jax.ShapeDtypeStruct((n, d), F32),
        scratch_shapes=[pltpu.VMEM((nhp, nk, t), F32), pltpu.VMEM((nhp, nk, t), F32),
                        pltpu.VMEM((nhp, PEER_TOPK, t), F32), pltpu.VMEM((nhp // 2, SUBLANE, t), F32),
                        pltpu.VMEM((sub, t), BF16), pltpu.VMEM((d, t), F32)],
        compiler_params=_cparams("parallel", "arbitrary"),
        name="peer",
    )(h1, xn_t, q_t, keys, u3, vt3)


def _final_kernel(h_ref, p_ref, gp_ref, wg_ref, wp_ref, gf_ref, y_ref):
    h = h_ref[...]
    gate = jax.nn.sigmoid(jnp.dot(_rms(h, gp_ref[...]).astype(BF16), wg_ref[...], preferred_element_type=F32))
    h = h + gate * jnp.dot(p_ref[...].astype(BF16), wp_ref[...], preferred_element_type=F32)
    y_ref[...] = _rms(h, gf_ref[...])


def _final(h2, p, g_ple, w_gate, w_proj, g_final):
    n, d = h2.shape
    tm = ROW_TILE
    pd = p.shape[1]
    row = lambda i: (i, 0)
    const = lambda i: (0, 0)
    return pl.pallas_call(
        _final_kernel,
        grid=(n // tm,),
        in_specs=[pl.BlockSpec((tm, d), row), pl.BlockSpec((tm, pd), row), pl.BlockSpec((1, d), const),
                  pl.BlockSpec(w_gate.shape, const), pl.BlockSpec(w_proj.shape, const), pl.BlockSpec((1, d), const)],
        out_specs=pl.BlockSpec((tm, d), row),
        out_shape=jax.ShapeDtypeStruct((n, d), F32),
        compiler_params=_cparams("parallel"),
        name="final",
    )(h2, p, g_ple, w_gate, w_proj, g_final)


def _rot_half_cols(w):
    d, width = w.shape
    w4 = w.reshape(d, width // HEAD_DIM, 2, HEAD_DIM // 2)
    return jnp.concatenate([-w4[:, :, 1:2], w4[:, :, 0:1]], axis=2).reshape(d, width)


def _block_diag(w):
    nb, bi, bj = w.shape
    eye = jnp.eye(nb, dtype=w.dtype)
    return (eye[:, None, :, None] * w[:, :, None, :]).reshape(nb * bi, nb * bj)


def _rope_tables(seq):
    half = HEAD_DIM // 2
    inv_freq = ROPE_THETA ** (-jnp.arange(half, dtype=F32) / half)
    ang = jnp.arange(seq, dtype=F32)[:, None] * inv_freq[None, :]
    reps = LANE // half
    return jnp.tile(jnp.cos(ang), (1, reps)), jnp.tile(jnp.sin(ang), (1, reps))


def _encoder(x, p, wts):
    b, s, d = x.shape
    n = b * s
    aw, lw = wts["aw"], wts["lw"]
    cos, sin = _rope_tables(s)
    x2 = x.reshape(n, d)
    q, k2, v2, xr, gr = _in_proj(x2, wts["mix_g"], wts["w_in"], cos, sin, s, aw, lw)
    attn_n = _attention(q, k2, v2, wts["sink"], wts["attn_g"], s)
    hf = _lru_fwd(xr, wts["conv_w"], wts["conv_b"], wts["wg"][0], wts["bg"][0], wts["lam"][0], b, s)
    lru_n = _lru_bwd(xr, wts["conv_w"], wts["conv_b"], wts["wg"][1], wts["bg"][1], wts["lam"][1],
                     gr, hf, wts["lru_g"], b, s)
    h1, xn_t, q_t = _out_proj(x2, attn_n, lru_n, wts["w_out"], wts["ffn_g"], wts["wq_t"])
    h2 = _peer(h1, xn_t, q_t, wts["keys"], wts["u3"], wts["vt3"])
    y = _final(h2, p.reshape(n, -1), wts["ple_g"], wts["w_gate"], wts["w_proj"], wts["final_g"])
    return y.reshape(b, s, d)


def kernel(x_prompt, x_sample, p_prompt, p_sample, mix_norm_g, w_in, attn_sink, conv_w, conv_b, lru_wa, lru_ba, lru_wx, lru_bx, lru_lambda, attn_out_norm_g, lru_out_norm_g, w_out, ffn_norm_g, peer_wq, peer_keys, peer_u, peer_v, ple_norm_g, ple_w_gate, ple_w_proj, final_norm_g):
    depth = w_in.shape[0]
    assert depth == 1, "single-layer encoder"
    l = 0
    d = w_in.shape[1]
    aw = N_Q_HEADS * HEAD_DIM
    kw = N_KV_HEADS * HEAD_DIM
    lw = conv_w.shape[2]
    assert kw == LANE and aw % LANE == 0
    o1, o2, o3, o4 = aw, aw + kw, aw + 2 * kw, aw + 2 * kw + lw
    w = w_in[l]
    wq_, wk_, wv_, wx_, wg_ = w[:, :o1], w[:, o1:o2], w[:, o2:o3], w[:, o3:o4], w[:, o4:]
    w_ext = jnp.concatenate([wq_, _rot_half_cols(wq_), wk_, _rot_half_cols(wk_), wv_, wx_, wg_], axis=1).astype(BF16)
    nh, _, nk, dk = peer_keys.shape[1:]
    n_exp = peer_u.shape[1]
    assert n_exp == nk * nk and nk == LANE and dk == LANE
    nblk = n_exp // PEER_SUB
    wts = dict(
        aw=aw, lw=lw,
        mix_g=mix_norm_g[l][None], w_in=w_ext, sink=attn_sink[l],
        attn_g=attn_out_norm_g[l][None], lru_g=lru_out_norm_g[l][None],
        conv_w=conv_w[l], conv_b=conv_b[l][None],
        wg=[jnp.concatenate([_block_diag(lru_wa[l, k]), _block_diag(lru_wx[l, k])], axis=1).astype(BF16) for k in range(2)],
        bg=[jnp.concatenate([lru_ba[l, k], lru_bx[l, k]])[None] for k in range(2)],
        lam=[lru_lambda[l, k][None] for k in range(2)],
        w_out=w_out[l].astype(BF16), ffn_g=ffn_norm_g[l][None],
        wq_t=peer_wq[l].T.astype(BF16),
        keys=peer_keys[l].reshape(nh * 2, nk, dk).astype(BF16),
        u3=peer_u[l].astype(BF16).reshape(nblk, PEER_SUB, d),
        vt3=peer_v[l].astype(BF16).reshape(nblk, PEER_SUB, d).transpose(0, 2, 1),
        ple_g=ple_norm_g[l][None], w_gate=ple_w_gate[l].astype(BF16), w_proj=ple_w_proj[l].astype(BF16),
        final_g=final_norm_g[None],
    )
    y_prompt = _encoder(x_prompt, p_prompt[l], wts)
    y_sample = _encoder(x_sample, p_sample[l], wts)
    return (y_prompt, y_sample)
```

```python
import functools

import jax
import jax.numpy as jnp
import numpy as np
from jax import lax
from jax.experimental import pallas as pl
from jax.experimental.pallas import tpu as pltpu

F32 = jnp.float32
BF16 = jnp.bfloat16

EPS = 1e-6
N_Q_HEADS = 8
N_KV_HEADS = 2
HEAD_DIM = 64
WINDOW = 128
ROPE_THETA = 10000.0
LRU_C = 8.0
CONV_LEFT = 2
PEER_TOPK = 16

LANE = 128
SUBLANE = 8
BF16_ROWS = 16
NOT_TOP = 64.0
ROW_TILE = 512
PEER_TOKENS = 512
PEER_SUB = SUBLANE * LANE
PEER_SUBS_PER_STEP = 2
VMEM_LIMIT = 56 * 1024 * 1024
NEG_INF = float("-inf")


def _cparams(*sem):
    return pltpu.CompilerParams(dimension_semantics=sem, vmem_limit_bytes=VMEM_LIMIT)


def _rms(x, g):
    return x * lax.rsqrt(jnp.mean(x * x, axis=-1, keepdims=True) + EPS) * g


def _in_proj_kernel(x_ref, g_ref, w_ref, cos_ref, sin_ref, q_ref, k_ref, v_ref, xr_ref, gr_ref):
    aw, kw, lw = q_ref.shape[1], LANE, xr_ref.shape[1]
    xn = _rms(x_ref[...], g_ref[...]).astype(BF16)
    z = jnp.dot(xn, w_ref[...], preferred_element_type=F32)
    c = cos_ref[...]
    s = sin_ref[...]
    reps = aw // LANE
    cq = jnp.concatenate([c] * reps, axis=1)
    sq = jnp.concatenate([s] * reps, axis=1)
    o = 0
    q = (z[:, o:o + aw] * cq + z[:, o + aw:o + 2 * aw] * sq) * (HEAD_DIM ** -0.5)
    o += 2 * aw
    k = z[:, o:o + kw] * c + z[:, o + kw:o + 2 * kw] * s
    o += 2 * kw
    v = z[:, o:o + kw]
    o += kw
    q_ref[...] = q.astype(BF16)
    k_ref[...] = jnp.concatenate([k, pltpu.roll(k, HEAD_DIM, 1)], axis=1).astype(BF16)
    v_ref[...] = jnp.concatenate([v, pltpu.roll(v, HEAD_DIM, 1)], axis=1).astype(BF16)
    xr_ref[...] = z[:, o:o + lw]
    gr_ref[...] = z[:, o + lw:o + 2 * lw]


def _in_proj(x, g, w_ext, cos, sin, seq, aw, lw):
    n, d = x.shape
    tm = ROW_TILE
    per_seq = seq // tm
    row = lambda i: (i, 0)
    const = lambda i: (0, 0)
    pos = lambda i: (i % per_seq, 0)
    return pl.pallas_call(
        _in_proj_kernel,
        grid=(n // tm,),
        in_specs=[pl.BlockSpec((tm, d), row), pl.BlockSpec((1, d), const),
                  pl.BlockSpec(w_ext.shape, const),
                  pl.BlockSpec((tm, LANE), pos), pl.BlockSpec((tm, LANE), pos)],
        out_specs=[pl.BlockSpec((tm, aw), row), pl.BlockSpec((tm, 2 * LANE), row),
                   pl.BlockSpec((tm, 2 * LANE), row), pl.BlockSpec((tm, lw), row),
                   pl.BlockSpec((tm, lw), row)],
        out_shape=[jax.ShapeDtypeStruct((n, aw), BF16), jax.ShapeDtypeStruct((n, 2 * LANE), BF16),
                   jax.ShapeDtypeStruct((n, 2 * LANE), BF16), jax.ShapeDtypeStruct((n, lw), F32),
                   jax.ShapeDtypeStruct((n, lw), F32)],
        compiler_params=_cparams("parallel"),
        name="in_proj",
    )(x, g, w_ext, cos, sin)


def _attn_kernel(sink_ref, q_ref, kp_ref, ks_ref, kn_ref, vp_ref, vs_ref, vn_ref, g_ref, o_ref, *, seq):
    qb = q_ref.shape[0]
    blk = WINDOW
    p0 = (pl.program_id(0) % (seq // qb)) * qb
    kf = jnp.concatenate([kp_ref[...], ks_ref[...], kn_ref[...]], axis=0)
    vf = jnp.concatenate([vp_ref[...], vs_ref[...], vn_ref[...]], axis=0)
    lane = lax.broadcasted_iota(jnp.int32, (blk, LANE), 1)
    row = lax.broadcasted_iota(jnp.int32, (blk, 3 * blk), 0)
    col = lax.broadcasted_iota(jnp.int32, (blk, 3 * blk), 1)
    rel = col - row
    band = (rel >= 0) & (rel <= 2 * WINDOW)
    grp = N_Q_HEADS // N_KV_HEADS
    for r in range(qb // blk):
        kpos = p0 + (r - 1) * blk + col
        valid = band & (kpos >= 0) & (kpos < seq)
        pairs = []
        for m in range(N_Q_HEADS // 2):
            qs = q_ref[r * blk:(r + 1) * blk, m * LANE:(m + 1) * LANE]
            halves = []
            for half in range(2):
                h = 2 * m + half
                kvh = h // grp
                sel = 0 if half == kvh else 1
                qm = jnp.where((lane >= HEAD_DIM) == (half == 1), qs, jnp.zeros_like(qs))
                kk = kf[r * blk:(r + 3) * blk, sel * LANE:(sel + 1) * LANE]
                vv = vf[r * blk:(r + 3) * blk, sel * LANE:(sel + 1) * LANE]
                s = lax.dot_general(qm, kk, (((1,), (1,)), ((), ())), preferred_element_type=F32)
                s = jnp.where(valid, s, -1e30)
                sk = sink_ref[h]
                mx = jnp.maximum(jnp.max(s, axis=1, keepdims=True), sk)
                e = jnp.exp(s - mx)
                den = jnp.sum(e, axis=1, keepdims=True) + jnp.exp(sk - mx)
                p = (e / den).astype(BF16)
                halves.append(jnp.dot(p, vv, preferred_element_type=F32))
            pairs.append(jnp.where(lane < HEAD_DIM, halves[0], halves[1]))
        attn = jnp.concatenate(pairs, axis=1)
        o_ref[r * blk:(r + 1) * blk, :] = _rms(attn, g_ref[...]).astype(BF16)


def _attention(q, k2, v2, sink, g, seq):
    n, aw = q.shape
    qb = ROW_TILE
    per = qb // WINDOW
    nblk = n // WINDOW
    row = lambda i: (i, 0)
    prev = lambda i: (jnp.maximum(i * per - 1, 0), 0)
    nxt = lambda i: (jnp.minimum((i + 1) * per, nblk - 1), 0)
    kw = k2.shape[1]
    return pl.pallas_call(
        functools.partial(_attn_kernel, seq=seq),
        grid=(n // qb,),
        in_specs=[pl.BlockSpec(memory_space=pltpu.SMEM),
                  pl.BlockSpec((qb, aw), row),
                  pl.BlockSpec((WINDOW, kw), prev), pl.BlockSpec((qb, kw), row), pl.BlockSpec((WINDOW, kw), nxt),
                  pl.BlockSpec((WINDOW, kw), prev), pl.BlockSpec((qb, kw), row), pl.BlockSpec((WINDOW, kw), nxt),
                  pl.BlockSpec((1, aw), lambda i: (0, 0))],
        out_specs=pl.BlockSpec((qb, aw), row),
        out_shape=jax.ShapeDtypeStruct((n, aw), BF16),
        compiler_params=_cparams("parallel"),
        name="attention",
    )(sink, q, k2, k2, k2, v2, v2, v2, g)


def _lru_gates(x_ref, xp_ref, xn_ref, cw_ref, cb_ref, wg_ref, bg_ref, lam_ref, ext_ref, a_ref, u_ref,
               has_prev, has_next):
    tb, lw = x_ref.shape
    x = x_ref[...]
    ext_ref[0:SUBLANE, :] = jnp.where(has_prev, xp_ref[...], 0.0)
    ext_ref[SUBLANE:SUBLANE + tb, :] = x
    ext_ref[SUBLANE + tb:2 * SUBLANE + tb, :] = jnp.where(has_next, xn_ref[...], 0.0)
    cw = cw_ref[...]
    xc = cb_ref[...] + cw[2:3] * x
    for j in (0, 1, 3):
        off = SUBLANE + j - CONV_LEFT
        xc = xc + cw[j:j + 1] * ext_ref[off:off + tb, :]
    gates = jnp.dot(xc.astype(BF16), wg_ref[...], preferred_element_type=F32) + bg_ref[...]
    gate_r = jax.nn.sigmoid(gates[:, :lw])
    gate_i = jax.nn.sigmoid(gates[:, lw:])
    lam = lam_ref[...]
    log_sig = jnp.minimum(lam, 0.0) - jnp.log1p(jnp.exp(-jnp.abs(lam)))
    log_a = LRU_C * gate_r * log_sig
    a = jnp.exp(log_a)
    a_ref[...] = a
    u_ref[...] = jnp.sqrt(-jnp.tanh(log_a) * (a * a + 1.0)) * (gate_i * xc)


def _lru_fwd_kernel(x_ref, xp_ref, xn_ref, cw_ref, cb_ref, wg_ref, bg_ref, lam_ref, h_ref,
                    ext_ref, a_ref, u_ref, carry_ref):
    j = pl.program_id(1)
    nt = pl.num_programs(1)
    tb = x_ref.shape[0]

    @pl.when(j == 0)
    def _():
        carry_ref[...] = jnp.zeros_like(carry_ref)

    _lru_gates(x_ref, xp_ref, xn_ref, cw_ref, cb_ref, wg_ref, bg_ref, lam_ref, ext_ref, a_ref, u_ref,
               j > 0, j < nt - 1)

    def body(t, h):
        h = a_ref[pl.ds(t, 1), :] * h + u_ref[pl.ds(t, 1), :]
        h_ref[pl.ds(t, 1), :] = h
        return h

    carry_ref[...] = lax.fori_loop(0, tb, body, carry_ref[...], unroll=8)


def _lru_bwd_kernel(x_ref, xp_ref, xn_ref, cw_ref, cb_ref, wg_ref, bg_ref, lam_ref, gr_ref, hf_ref, g_ref,
                    o_ref, ext_ref, a_ref, u_ref, hb_ref, carry_ref):
    j = pl.program_id(1)
    nt = pl.num_programs(1)
    tb = x_ref.shape[0]

    @pl.when(j == 0)
    def _():
        carry_ref[...] = jnp.zeros_like(carry_ref)

    _lru_gates(x_ref, xp_ref, xn_ref, cw_ref, cb_ref, wg_ref, bg_ref, lam_ref, ext_ref, a_ref, u_ref,
               j < nt - 1, j > 0)

    def body(i, h):
        t = tb - 1 - i
        h = a_ref[pl.ds(t, 1), :] * h + u_ref[pl.ds(t, 1), :]
        hb_ref[pl.ds(t, 1), :] = h
        return h

    carry_ref[...] = lax.fori_loop(0, tb, body, carry_ref[...], unroll=8)
    lru = jax.nn.gelu(gr_ref[...]) * (hf_ref[...] + hb_ref[...])
    o_ref[...] = _rms(lru, g_ref[...]).astype(BF16)


def _lru_specs(batch, seq, lw, tb, reverse):
    nt = seq // tb
    per8 = tb // SUBLANE
    n8 = batch * seq // SUBLANE

    def blk(b, j):
        return b * nt + (nt - 1 - j if reverse else j)

    main = lambda b, j: (blk(b, j), 0)
    prev = lambda b, j: (jnp.maximum(blk(b, j) * per8 - 1, 0), 0)
    nxt = lambda b, j: (jnp.minimum((blk(b, j) + 1) * per8, n8 - 1), 0)
    const = lambda b, j: (0, 0)
    specs = [pl.BlockSpec((tb, lw), main), pl.BlockSpec((SUBLANE, lw), prev), pl.BlockSpec((SUBLANE, lw), nxt),
             pl.BlockSpec((4, lw), const), pl.BlockSpec((1, lw), const),
             pl.BlockSpec((lw, 2 * lw), const), pl.BlockSpec((1, 2 * lw), const), pl.BlockSpec((1, lw), const)]
    return specs, main, const, nt


def _lru_fwd(xr, cw, cb, wg, bg, lam, batch, seq):
    n, lw = xr.shape
    tb = ROW_TILE
    specs, main, _, nt = _lru_specs(batch, seq, lw, tb, reverse=False)
    return pl.pallas_call(
        _lru_fwd_kernel,
        grid=(batch, nt),
        in_specs=specs,
        out_specs=pl.BlockSpec((tb, lw), main),
        out_shape=jax.ShapeDtypeStruct((n, lw), F32),
        scratch_shapes=[pltpu.VMEM((tb + 2 * SUBLANE, lw), F32), pltpu.VMEM((tb, lw), F32),
                        pltpu.VMEM((tb, lw), F32), pltpu.VMEM((1, lw), F32)],
        compiler_params=_cparams("parallel", "arbitrary"),
        name="lru_fwd",
    )(xr, xr, xr, cw, cb, wg, bg, lam)


def _lru_bwd(xr, cw, cb, wg, bg, lam, gr, hf, g, batch, seq):
    n, lw = xr.shape
    tb = ROW_TILE
    specs, main, const, nt = _lru_specs(batch, seq, lw, tb, reverse=True)
    specs = specs + [pl.BlockSpec((tb, lw), main), pl.BlockSpec((tb, lw), main), pl.BlockSpec((1, lw), const)]
    return pl.pallas_call(
        _lru_bwd_kernel,
        grid=(batch, nt),
        in_specs=specs,
        out_specs=pl.BlockSpec((tb, lw), main),
        out_shape=jax.ShapeDtypeStruct((n, lw), BF16),
        scratch_shapes=[pltpu.VMEM((tb + 2 * SUBLANE, lw), F32), pltpu.VMEM((tb, lw), F32),
                        pltpu.VMEM((tb, lw), F32), pltpu.VMEM((tb, lw), F32), pltpu.VMEM((1, lw), F32)],
        compiler_params=_cparams("parallel", "arbitrary"),
        name="lru_bwd",
    )(xr, xr, xr, cw, cb, wg, bg, lam, gr, hf, g)


def _out_proj_kernel(x_ref, at_ref, lr_ref, wo_ref, g_ref, wqt_ref, h1_ref, xnt_ref, qt_ref):
    aw = at_ref.shape[1]
    h1 = (x_ref[...] + jnp.dot(at_ref[...], wo_ref[0:aw, :], preferred_element_type=F32)
          + jnp.dot(lr_ref[...], wo_ref[aw:, :], preferred_element_type=F32))
    h1_ref[...] = h1
    xnt = _rms(h1, g_ref[...]).T.astype(BF16)
    xnt_ref[0] = xnt
    qt_ref[0] = jnp.dot(wqt_ref[...], xnt, preferred_element_type=F32).astype(BF16)


def _out_proj(x, attn_n, lru_n, w_out, g, wq_t):
    n, d = x.shape
    tm = PEER_TOKENS
    aw, lw = attn_n.shape[1], lru_n.shape[1]
    qd = wq_t.shape[0]
    row = lambda i: (i, 0)
    const = lambda i: (0, 0)
    tile = lambda i: (i, 0, 0)
    return pl.pallas_call(
        _out_proj_kernel,
        grid=(n // tm,),
        in_specs=[pl.BlockSpec((tm, d), row), pl.BlockSpec((tm, aw), row), pl.BlockSpec((tm, lw), row),
                  pl.BlockSpec(w_out.shape, const), pl.BlockSpec((1, d), const), pl.BlockSpec(wq_t.shape, const)],
        out_specs=[pl.BlockSpec((tm, d), row), pl.BlockSpec((1, d, tm), tile), pl.BlockSpec((1, qd, tm), tile)],
        out_shape=[jax.ShapeDtypeStruct((n, d), F32), jax.ShapeDtypeStruct((n // tm, d, tm), BF16),
                   jax.ShapeDtypeStruct((n // tm, qd, tm), BF16)],
        compiler_params=_cparams("parallel"),
        name="out_proj",
    )(x, attn_n, lru_n, w_out, g, wq_t)


def _peer_select(qt_ref, keys_ref, s_ref, a_ref, e1_ref, ell_ref, rank_ref, e2_ref):
    nhp = keys_ref.shape[0]
    nh = nhp // 2
    t = qt_ref.shape[2]
    nch = t // LANE
    dk = keys_ref.shape[2]

    def scores(hp, _):
        q = qt_ref[0, pl.ds(pl.multiple_of(hp * dk, dk), dk), :]
        s_ref[hp] = jnp.dot(keys_ref[hp], q, preferred_element_type=F32)
        return 0

    lax.fori_loop(0, nhp, scores, 0)

    def topk(h, _):
        for half in range(2):
            hp = 2 * h + half
            for c in range(nch):
                cs = slice(c * LANE, (c + 1) * LANE)
                s0 = s_ref[hp, :, cs]
                s = s0
                rank = jnp.full(s0.shape, NOT_TOP, F32)
                for k in range(PEER_TOPK):
                    m = jnp.max(s, axis=0, keepdims=True)
                    a_ref[hp, k:k + 1, cs] = m
                    if k == 0:
                        e = jnp.exp(s0 - m)
                    hit = s == m
                    rank = jnp.where(hit, float(k), rank)
                    s = jnp.where(hit, NEG_INF, s)
                if half == 0:
                    e1_ref[h, :, cs] = e
                    ell_ref[h, :, cs] = rank
                else:
                    e2_ref[h, :, cs] = e.astype(BF16)
                    rank_ref[h, :, cs] = rank.astype(BF16)
        return 0

    lax.fori_loop(0, nh, topk, 0)

    rows8 = lax.broadcasted_iota(jnp.int32, (SUBLANE, LANE), 0)

    def cands(h, _):
        for c in range(nch):
            cs = slice(c * LANE, (c + 1) * LANE)
            a = a_ref[2 * h, :, cs]
            b = a_ref[2 * h + 1, :, cs]
            a_lo, b_lo = a[0:SUBLANE], b[0:SUBLANE]
            blocks = [a[0:1] + b_lo, a[0:1] + b[SUBLANE:], a[SUBLANE:] + b[0:1]]
            blocks += [a[i:i + 1] + b_lo for i in (1, 2, 3)]
            blocks += [jnp.where(rows8 >= 4, a_lo + b[j:j + 1], NEG_INF) for j in (0, 1, 2)]
            top = a[0:1] + b[0:1]
            z = jnp.zeros((1, LANE), F32)
            m = top
            for k in range(PEER_TOPK):
                m = blocks[0]
                for blk in blocks[1:]:
                    m = jnp.maximum(m, blk)
                m = jnp.max(m, axis=0, keepdims=True)
                z = z + jnp.exp(m - top)
                if k + 1 < PEER_TOPK:
                    blocks = [jnp.where(blk == m, NEG_INF, blk) for blk in blocks]
            tau = m
            e1_ref[h, :, cs] = e1_ref[h, :, cs] * (1.0 / z)
            rank1 = ell_ref[h, :, cs]
            ell = jnp.zeros(rank1.shape, F32)
            for r in range(PEER_TOPK):
                cnt = jnp.sum(jnp.where(a[r:r + 1] + b >= tau, 1.0, 0.0), axis=0, keepdims=True)
                ell = jnp.where(rank1 == float(r), cnt, ell)
            ell_ref[h, :, cs] = ell
        return 0

    lax.fori_loop(0, nh, cands, 0)


def _peer_kernel(h1_ref, xnt_ref, qt_ref, keys_ref, u_ref, vt_ref, o_ref,
                 s_ref, a_ref, e1_ref, ell_ref, rank_ref, e2_ref, p_ref, acc_ref):
    eb = pl.program_id(1)
    nsub = u_ref.shape[0]
    sub = u_ref.shape[1]
    nk = keys_ref.shape[1]
    nh = keys_ref.shape[0] // 2
    t = xnt_ref.shape[2]
    nch = t // LANE
    assert sub == SUBLANE * nk
    ngrp = nk // BF16_ROWS

    @pl.when(eb == 0)
    def _():
        _peer_select(qt_ref, keys_ref, s_ref, a_ref, e1_ref, ell_ref, rank_ref, e2_ref)
        acc_ref[...] = jnp.zeros_like(acc_ref)

    def sub_block(sb, _):
        act = jnp.dot(u_ref[sb], xnt_ref[0], preferred_element_type=F32)
        i0 = pl.multiple_of((eb * nsub + sb) * SUBLANE, SUBLANE)
        for c in range(nch):
            cs = slice(c * LANE, (c + 1) * LANE)
            ell8 = [ell_ref[h, pl.ds(i0, SUBLANE), cs] for h in range(nh)]
            e18 = [e1_ref[h, pl.ds(i0, SUBLANE), cs] for h in range(nh)]
            for ii in range(SUBLANE):
                w = [jnp.zeros((BF16_ROWS, LANE), BF16) for _ in range(ngrp)]
                for h in range(nh):
                    ell = jnp.broadcast_to(ell8[h][ii:ii + 1], (BF16_ROWS, LANE)).astype(BF16)
                    e1 = jnp.broadcast_to(e18[h][ii:ii + 1], (BF16_ROWS, LANE)).astype(BF16)
                    for g in range(ngrp):
                        gs = slice(g * BF16_ROWS, (g + 1) * BF16_ROWS)
                        hit = rank_ref[h, gs, cs] < ell
                        w[g] = w[g] + jnp.where(hit, e2_ref[h, gs, cs] * e1, jnp.zeros_like(e1))
                ga = jax.nn.gelu(act[ii * nk:(ii + 1) * nk, cs]).astype(BF16)
                for g in range(ngrp):
                    gs = slice(g * BF16_ROWS, (g + 1) * BF16_ROWS)
                    p_ref[ii * nk + g * BF16_ROWS:ii * nk + (g + 1) * BF16_ROWS, cs] = w[g] * ga[gs]
        acc_ref[...] += jnp.dot(vt_ref[sb], p_ref[...], preferred_element_type=F32)
        return 0

    lax.fori_loop(0, nsub, sub_block, 0)

    @pl.when(eb == pl.num_programs(1) - 1)
    def _():
        o_ref[...] = h1_ref[...] + acc_ref[...].T


def _peer(h1, xn_t, q_t, keys, u3, vt3):
    n, d = h1.shape
    nt, _, t = xn_t.shape
    nblk, sub, _ = u3.shape
    nsub = PEER_SUBS_PER_STEP
    nhp, nk, dk = keys.shape
    tile = lambda i, e: (i, 0, 0)
    return pl.pallas_call(
        _peer_kernel,
        grid=(nt, nblk // nsub),
        in_specs=[pl.BlockSpec((t, d), lambda i, e: (i, 0)),
                  pl.BlockSpec((1, d, t), tile), pl.BlockSpec((1, q_t.shape[1], t), tile),
                  pl.BlockSpec(keys.shape, lambda i, e: (0, 0, 0)),
                  pl.BlockSpec((nsub, sub, d), lambda i, e: (e, 0, 0)),
                  pl.BlockSpec((nsub, d, sub), lambda i, e: (e, 0, 0))],
        out_specs=pl.BlockSpec((t, d), lambda i, e: (i, 0)),
        out_shape=jax.ShapeDtypeStruct((n, d), F32),
        scratch_shapes=[pltpu.VMEM((nhp, nk, t), F32), pltpu.VMEM((nhp, PEER_TOPK, t), F32),
                        pltpu.VMEM((nhp // 2, nk, t), F32), pltpu.VMEM((nhp // 2, nk, t), F32),
                        pltpu.VMEM((nhp // 2, nk, t), BF16), pltpu.VMEM((nhp // 2, nk, t), BF16),
                        pltpu.VMEM((sub, t), BF16), pltpu.VMEM((d, t), F32)],
        compiler_params=_cparams("parallel", "arbitrary"),
        name="peer",
    )(h1, xn_t, q_t, keys, u3, vt3)


def _final_kernel(h_ref, p_ref, gp_ref, wg_ref, wp_ref, gf_ref, y_ref):
    h = h_ref[...]
    gate = jax.nn.sigmoid(jnp.dot(_rms(h, gp_ref[...]).astype(BF16), wg_ref[...], preferred_element_type=F32))
    h = h + gate * jnp.dot(p_ref[...].astype(BF16), wp_ref[...], preferred_element_type=F32)
    y_ref[...] = _rms(h, gf_ref[...])


def _final(h2, p, g_ple, w_gate, w_proj, g_final):
    n, d = h2.shape
    tm = ROW_TILE
    pd = p.shape[1]
    row = lambda i: (i, 0)
    const = lambda i: (0, 0)
    return pl.pallas_call(
        _final_kernel,
        grid=(n // tm,),
        in_specs=[pl.BlockSpec((tm, d), row), pl.BlockSpec((tm, pd), row), pl.BlockSpec((1, d), const),
                  pl.BlockSpec(w_gate.shape, const), pl.BlockSpec(w_proj.shape, const), pl.BlockSpec((1, d), const)],
        out_specs=pl.BlockSpec((tm, d), row),
        out_shape=jax.ShapeDtypeStruct((n, d), F32),
        compiler_params=_cparams("parallel"),
        name="final",
    )(h2, p, g_ple, w_gate, w_proj, g_final)


def _rot_half_cols(w):
    d, width = w.shape
    w4 = w.reshape(d, width // HEAD_DIM, 2, HEAD_DIM // 2)
    return jnp.concatenate([-w4[:, :, 1:2], w4[:, :, 0:1]], axis=2).reshape(d, width)


def _block_diag(w):
    nb, bi, bj = w.shape
    eye = jnp.eye(nb, dtype=w.dtype)
    return (eye[:, None, :, None] * w[:, :, None, :]).reshape(nb * bi, nb * bj)


def _rope_tables(seq):
    half = HEAD_DIM // 2
    inv_freq = ROPE_THETA ** (-jnp.arange(half, dtype=F32) / half)
    ang = jnp.arange(seq, dtype=F32)[:, None] * inv_freq[None, :]
    reps = LANE // half
    return jnp.tile(jnp.cos(ang), (1, reps)), jnp.tile(jnp.sin(ang), (1, reps))


def _encoder(x, p, wts):
    b, s, d = x.shape
    n = b * s
    aw, lw = wts["aw"], wts["lw"]
    cos, sin = _rope_tables(s)
    x2 = x.reshape(n, d)
    q, k2, v2, xr, gr = _in_proj(x2, wts["mix_g"], wts["w_in"], cos, sin, s, aw, lw)
    attn_n = _attention(q, k2, v2, wts["sink"], wts["attn_g"], s)
    hf = _lru_fwd(xr, wts["conv_w"], wts["conv_b"], wts["wg"][0], wts["bg"][0], wts["lam"][0], b, s)
    lru_n = _lru_bwd(xr, wts["conv_w"], wts["conv_b"], wts["wg"][1], wts["bg"][1], wts["lam"][1],
                     gr, hf, wts["lru_g"], b, s)
    h1, xn_t, q_t = _out_proj(x2, attn_n, lru_n, wts["w_out"], wts["ffn_g"], wts["wq_t"])
    h2 = _peer(h1, xn_t, q_t, wts["keys"], wts["u3"], wts["vt3"])
    y = _final(h2, p.reshape(n, -1), wts["ple_g"], wts["w_gate"], wts["w_proj"], wts["final_g"])
    return y.reshape(b, s, d)


def kernel(x_prompt, x_sample, p_prompt, p_sample, mix_norm_g, w_in, attn_sink, conv_w, conv_b, lru_wa, lru_ba, lru_wx, lru_bx, lru_lambda, attn_out_norm_g, lru_out_norm_g, w_out, ffn_norm_g, peer_wq, peer_keys, peer_u, peer_v, ple_norm_g, ple_w_gate, ple_w_proj, final_norm_g):
    depth = w_in.shape[0]
    assert depth == 1, "single-layer encoder"
    l = 0
    d = w_in.shape[1]
    aw = N_Q_HEADS * HEAD_DIM
    kw = N_KV_HEADS * HEAD_DIM
    lw = conv_w.shape[2]
    assert kw == LANE and aw % LANE == 0
    o1, o2, o3, o4 = aw, aw + kw, aw + 2 * kw, aw + 2 * kw + lw
    w = w_in[l]
    wq_, wk_, wv_, wx_, wg_ = w[:, :o1], w[:, o1:o2], w[:, o2:o3], w[:, o3:o4], w[:, o4:]
    w_ext = jnp.concatenate([wq_, _rot_half_cols(wq_), wk_, _rot_half_cols(wk_), wv_, wx_, wg_], axis=1).astype(BF16)
    nh, _, nk, dk = peer_keys.shape[1:]
    n_exp = peer_u.shape[1]
    assert n_exp == nk * nk and nk == LANE and dk == LANE
    nblk = n_exp // PEER_SUB
    wts = dict(
        aw=aw, lw=lw,
        mix_g=mix_norm_g[l][None], w_in=w_ext, sink=attn_sink[l],
        attn_g=attn_out_norm_g[l][None], lru_g=lru_out_norm_g[l][None],
        conv_w=conv_w[l], conv_b=conv_b[l][None],
        wg=[jnp.concatenate([_block_diag(lru_wa[l, k]), _block_diag(lru_wx[l, k])], axis=1).astype(BF16) for k in range(2)],
        bg=[jnp.concatenate([lru_ba[l, k], lru_bx[l, k]])[None] for k in range(2)],
        lam=[lru_lambda[l, k][None] for k in range(2)],
        w_out=w_out[l].astype(BF16), ffn_g=ffn_norm_g[l][None],
        wq_t=peer_wq[l].T.astype(BF16),
        keys=peer_keys[l].reshape(nh * 2, nk, dk).astype(BF16),
        u3=peer_u[l].astype(BF16).reshape(nblk, PEER_SUB, d),
        vt3=peer_v[l].astype(BF16).reshape(nblk, PEER_SUB, d).transpose(0, 2, 1),
        ple_g=ple_norm_g[l][None], w_gate=ple_w_gate[l].astype(BF16), w_proj=ple_w_proj[l].astype(BF16),
        final_g=final_norm_g[None],
    )
    y_prompt = _encoder(x_prompt, p_prompt[l], wts)
    y_sample = _encoder(x_sample, p_sample[l], wts)
    return (y_prompt, y_sample)
```

```python
import functools
import math

import jax
import jax.numpy as jnp
import numpy as np
from jax import lax
from jax.experimental import pallas as pl
from jax.experimental.pallas import tpu as pltpu

F32 = jnp.float32
BF16 = jnp.bfloat16

EPS = 1e-6
N_Q_HEADS = 8
N_KV_HEADS = 2
HEAD_DIM = 64
WINDOW = 128
ROPE_THETA = 10000.0
LRU_C = 8.0
CONV_LEFT = 2
PEER_TOPK = 16

LANE = 128
SUBLANE = 8
BF16_ROWS = 16
NOT_TOP = 64.0
ROW_TILE = 512
PEER_TOKENS = 512
PEER_SUB = 512
PEER_SUBS_PER_STEP = 4
VMEM_LIMIT = 56 * 1024 * 1024
NEG_INF = float("-inf")


def _cparams(*sem):
    return pltpu.CompilerParams(dimension_semantics=sem, vmem_limit_bytes=VMEM_LIMIT)


def _rms(x, g):
    return x * lax.rsqrt(jnp.mean(x * x, axis=-1, keepdims=True) + EPS) * g


_GELU_K0 = -2.0 * math.sqrt(2.0 / math.pi) * math.log2(math.e)
_GELU_K1 = _GELU_K0 * 0.044715


def _gelu_tanh(x):
    return x / (1.0 + jnp.exp2(x * (_GELU_K0 + _GELU_K1 * (x * x))))


def _in_proj_kernel(x_ref, g_ref, w_ref, cos_ref, sin_ref, q_ref, k_ref, v_ref, xr_ref, gr_ref):
    aw, kw, lw = q_ref.shape[1], LANE, xr_ref.shape[1]
    xn = _rms(x_ref[...], g_ref[...]).astype(BF16)
    z = jnp.dot(xn, w_ref[...], preferred_element_type=F32)
    c = cos_ref[...]
    s = sin_ref[...]
    reps = aw // LANE
    cq = jnp.concatenate([c] * reps, axis=1)
    sq = jnp.concatenate([s] * reps, axis=1)
    o = 0
    q = (z[:, o:o + aw] * cq + z[:, o + aw:o + 2 * aw] * sq) * (HEAD_DIM ** -0.5)
    o += 2 * aw
    k = z[:, o:o + kw] * c + z[:, o + kw:o + 2 * kw] * s
    o += 2 * kw
    v = z[:, o:o + kw]
    o += kw
    q_ref[...] = q.astype(BF16)
    k_ref[...] = jnp.concatenate([k, pltpu.roll(k, HEAD_DIM, 1)], axis=1).astype(BF16)
    v_ref[...] = jnp.concatenate([v, pltpu.roll(v, HEAD_DIM, 1)], axis=1).astype(BF16)
    xr_ref[...] = z[:, o:o + lw]
    gr_ref[...] = z[:, o + lw:o + 2 * lw]


def _in_proj(x, g, w_ext, cos, sin, seq, aw, lw):
    n, d = x.shape
    tm = ROW_TILE
    per_seq = seq // tm
    row = lambda i: (i, 0)
    const = lambda i: (0, 0)
    pos = lambda i: (i % per_seq, 0)
    return pl.pallas_call(
        _in_proj_kernel,
        grid=(n // tm,),
        in_specs=[pl.BlockSpec((tm, d), row), pl.BlockSpec((1, d), const),
                  pl.BlockSpec(w_ext.shape, const),
                  pl.BlockSpec((tm, LANE), pos), pl.BlockSpec((tm, LANE), pos)],
        out_specs=[pl.BlockSpec((tm, aw), row), pl.BlockSpec((tm, 2 * LANE), row),
                   pl.BlockSpec((tm, 2 * LANE), row), pl.BlockSpec((tm, lw), row),
                   pl.BlockSpec((tm, lw), row)],
        out_shape=[jax.ShapeDtypeStruct((n, aw), BF16), jax.ShapeDtypeStruct((n, 2 * LANE), BF16),
                   jax.ShapeDtypeStruct((n, 2 * LANE), BF16), jax.ShapeDtypeStruct((n, lw), F32),
                   jax.ShapeDtypeStruct((n, lw), F32)],
        compiler_params=_cparams("parallel"),
        name="in_proj",
    )(x, g, w_ext, cos, sin)


def _attn_kernel(sink_ref, q_ref, kp_ref, ks_ref, kn_ref, vp_ref, vs_ref, vn_ref, g_ref, o_ref, *, seq):
    qb = q_ref.shape[0]
    blk = WINDOW
    p0 = (pl.program_id(0) % (seq // qb)) * qb
    kf = jnp.concatenate([kp_ref[...], ks_ref[...], kn_ref[...]], axis=0)
    vf = jnp.concatenate([vp_ref[...], vs_ref[...], vn_ref[...]], axis=0)
    lane = lax.broadcasted_iota(jnp.int32, (blk, LANE), 1)
    row = lax.broadcasted_iota(jnp.int32, (blk, 3 * blk), 0)
    col = lax.broadcasted_iota(jnp.int32, (blk, 3 * blk), 1)
    rel = col - row
    band = (rel >= 0) & (rel <= 2 * WINDOW)
    grp = N_Q_HEADS // N_KV_HEADS
    for r in range(qb // blk):
        kpos = p0 + (r - 1) * blk + col
        valid = band & (kpos >= 0) & (kpos < seq)
        pairs = []
        for m in range(N_Q_HEADS // 2):
            qs = q_ref[r * blk:(r + 1) * blk, m * LANE:(m + 1) * LANE]
            halves = []
            for half in range(2):
                h = 2 * m + half
                kvh = h // grp
                sel = 0 if half == kvh else 1
                qm = jnp.where((lane >= HEAD_DIM) == (half == 1), qs, jnp.zeros_like(qs))
                kk = kf[r * blk:(r + 3) * blk, sel * LANE:(sel + 1) * LANE]
                vv = vf[r * blk:(r + 3) * blk, sel * LANE:(sel + 1) * LANE]
                s = lax.dot_general(qm, kk, (((1,), (1,)), ((), ())), preferred_element_type=F32)
                s = jnp.where(valid, s, -1e30)
                sk = sink_ref[h]
                mx = jnp.maximum(jnp.max(s, axis=1, keepdims=True), sk)
                e = jnp.exp(s - mx)
                den = jnp.sum(e, axis=1, keepdims=True) + jnp.exp(sk - mx)
                p = (e / den).astype(BF16)
                halves.append(jnp.dot(p, vv, preferred_element_type=F32))
            pairs.append(jnp.where(lane < HEAD_DIM, halves[0], halves[1]))
        attn = jnp.concatenate(pairs, axis=1)
        o_ref[r * blk:(r + 1) * blk, :] = _rms(attn, g_ref[...]).astype(BF16)


def _attention(q, k2, v2, sink, g, seq):
    n, aw = q.shape
    qb = ROW_TILE
    per = qb // WINDOW
    nblk = n // WINDOW
    row = lambda i: (i, 0)
    prev = lambda i: (jnp.maximum(i * per - 1, 0), 0)
    nxt = lambda i: (jnp.minimum((i + 1) * per, nblk - 1), 0)
    kw = k2.shape[1]
    return pl.pallas_call(
        functools.partial(_attn_kernel, seq=seq),
        grid=(n // qb,),
        in_specs=[pl.BlockSpec(memory_space=pltpu.SMEM),
                  pl.BlockSpec((qb, aw), row),
                  pl.BlockSpec((WINDOW, kw), prev), pl.BlockSpec((qb, kw), row), pl.BlockSpec((WINDOW, kw), nxt),
                  pl.BlockSpec((WINDOW, kw), prev), pl.BlockSpec((qb, kw), row), pl.BlockSpec((WINDOW, kw), nxt),
                  pl.BlockSpec((1, aw), lambda i: (0, 0))],
        out_specs=pl.BlockSpec((qb, aw), row),
        out_shape=jax.ShapeDtypeStruct((n, aw), BF16),
        compiler_params=_cparams("parallel"),
        name="attention",
    )(sink, q, k2, k2, k2, v2, v2, v2, g)


def _lru_gates(x_ref, xp_ref, xn_ref, cw_ref, cb_ref, wg_ref, bg_ref, lam_ref, ext_ref, a_ref, u_ref,
               has_prev, has_next):
    tb, lw = x_ref.shape
    x = x_ref[...]
    ext_ref[0:SUBLANE, :] = jnp.where(has_prev, xp_ref[...], 0.0)
    ext_ref[SUBLANE:SUBLANE + tb, :] = x
    ext_ref[SUBLANE + tb:2 * SUBLANE + tb, :] = jnp.where(has_next, xn_ref[...], 0.0)
    cw = cw_ref[...]
    xc = cb_ref[...] + cw[2:3] * x
    for j in (0, 1, 3):
        off = SUBLANE + j - CONV_LEFT
        xc = xc + cw[j:j + 1] * ext_ref[off:off + tb, :]
    gates = jnp.dot(xc.astype(BF16), wg_ref[...], preferred_element_type=F32) + bg_ref[...]
    gate_r = jax.nn.sigmoid(gates[:, :lw])
    gate_i = jax.nn.sigmoid(gates[:, lw:])
    lam = lam_ref[...]
    log_sig = jnp.minimum(lam, 0.0) - jnp.log1p(jnp.exp(-jnp.abs(lam)))
    log_a = LRU_C * gate_r * log_sig
    a = jnp.exp(log_a)
    a_ref[...] = a
    u_ref[...] = jnp.sqrt(-jnp.tanh(log_a) * (a * a + 1.0)) * (gate_i * xc)


def _lru_fwd_kernel(x_ref, xp_ref, xn_ref, cw_ref, cb_ref, wg_ref, bg_ref, lam_ref, h_ref,
                    ext_ref, a_ref, u_ref, carry_ref):
    j = pl.program_id(1)
    nt = pl.num_programs(1)
    tb = x_ref.shape[0]

    @pl.when(j == 0)
    def _():
        carry_ref[...] = jnp.zeros_like(carry_ref)

    _lru_gates(x_ref, xp_ref, xn_ref, cw_ref, cb_ref, wg_ref, bg_ref, lam_ref, ext_ref, a_ref, u_ref,
               j > 0, j < nt - 1)

    def body(t, h):
        h = a_ref[pl.ds(t, 1), :] * h + u_ref[pl.ds(t, 1), :]
        h_ref[pl.ds(t, 1), :] = h
        return h

    carry_ref[...] = lax.fori_loop(0, tb, body, carry_ref[...], unroll=8)


def _lru_bwd_kernel(x_ref, xp_ref, xn_ref, cw_ref, cb_ref, wg_ref, bg_ref, lam_ref, gr_ref, hf_ref, g_ref,
                    o_ref, ext_ref, a_ref, u_ref, hb_ref, carry_ref):
    j = pl.program_id(1)
    nt = pl.num_programs(1)
    tb = x_ref.shape[0]

    @pl.when(j == 0)
    def _():
        carry_ref[...] = jnp.zeros_like(carry_ref)

    _lru_gates(x_ref, xp_ref, xn_ref, cw_ref, cb_ref, wg_ref, bg_ref, lam_ref, ext_ref, a_ref, u_ref,
               j < nt - 1, j > 0)

    def body(i, h):
        t = tb - 1 - i
        h = a_ref[pl.ds(t, 1), :] * h + u_ref[pl.ds(t, 1), :]
        hb_ref[pl.ds(t, 1), :] = h
        return h

    carry_ref[...] = lax.fori_loop(0, tb, body, carry_ref[...], unroll=8)
    lru = jax.nn.gelu(gr_ref[...]) * (hf_ref[...] + hb_ref[...])
    o_ref[...] = _rms(lru, g_ref[...]).astype(BF16)


def _lru_specs(batch, seq, lw, tb, reverse):
    nt = seq // tb
    per8 = tb // SUBLANE
    n8 = batch * seq // SUBLANE

    def blk(b, j):
        return b * nt + (nt - 1 - j if reverse else j)

    main = lambda b, j: (blk(b, j), 0)
    prev = lambda b, j: (jnp.maximum(blk(b, j) * per8 - 1, 0), 0)
    nxt = lambda b, j: (jnp.minimum((blk(b, j) + 1) * per8, n8 - 1), 0)
    const = lambda b, j: (0, 0)
    specs = [pl.BlockSpec((tb, lw), main), pl.BlockSpec((SUBLANE, lw), prev), pl.BlockSpec((SUBLANE, lw), nxt),
             pl.BlockSpec((4, lw), const), pl.BlockSpec((1, lw), const),
             pl.BlockSpec((lw, 2 * lw), const), pl.BlockSpec((1, 2 * lw), const), pl.BlockSpec((1, lw), const)]
    return specs, main, const, nt


def _lru_fwd(xr, cw, cb, wg, bg, lam, batch, seq):
    n, lw = xr.shape
    tb = ROW_TILE
    specs, main, _, nt = _lru_specs(batch, seq, lw, tb, reverse=False)
    return pl.pallas_call(
        _lru_fwd_kernel,
        grid=(batch, nt),
        in_specs=specs,
        out_specs=pl.BlockSpec((tb, lw), main),
        out_shape=jax.ShapeDtypeStruct((n, lw), F32),
        scratch_shapes=[pltpu.VMEM((tb + 2 * SUBLANE, lw), F32), pltpu.VMEM((tb, lw), F32),
                        pltpu.VMEM((tb, lw), F32), pltpu.VMEM((1, lw), F32)],
        compiler_params=_cparams("parallel", "arbitrary"),
        name="lru_fwd",
    )(xr, xr, xr, cw, cb, wg, bg, lam)


def _lru_bwd(xr, cw, cb, wg, bg, lam, gr, hf, g, batch, seq):
    n, lw = xr.shape
    tb = ROW_TILE
    specs, main, const, nt = _lru_specs(batch, seq, lw, tb, reverse=True)
    specs = specs + [pl.BlockSpec((tb, lw), main), pl.BlockSpec((tb, lw), main), pl.BlockSpec((1, lw), const)]
    return pl.pallas_call(
        _lru_bwd_kernel,
        grid=(batch, nt),
        in_specs=specs,
        out_specs=pl.BlockSpec((tb, lw), main),
        out_shape=jax.ShapeDtypeStruct((n, lw), BF16),
        scratch_shapes=[pltpu.VMEM((tb + 2 * SUBLANE, lw), F32), pltpu.VMEM((tb, lw), F32),
                        pltpu.VMEM((tb, lw), F32), pltpu.VMEM((tb, lw), F32), pltpu.VMEM((1, lw), F32)],
        compiler_params=_cparams("parallel", "arbitrary"),
        name="lru_bwd",
    )(xr, xr, xr, cw, cb, wg, bg, lam, gr, hf, g)


def _out_proj_kernel(x_ref, at_ref, lr_ref, wo_ref, g_ref, wqt_ref, h1_ref, xnt_ref, qt_ref):
    aw = at_ref.shape[1]
    h1 = (x_ref[...] + jnp.dot(at_ref[...], wo_ref[0:aw, :], preferred_element_type=F32)
          + jnp.dot(lr_ref[...], wo_ref[aw:, :], preferred_element_type=F32))
    h1_ref[...] = h1
    xnt = _rms(h1, g_ref[...]).T.astype(BF16)
    xnt_ref[0] = xnt
    qt_ref[0] = jnp.dot(wqt_ref[...], xnt, preferred_element_type=F32).astype(BF16)


def _out_proj(x, attn_n, lru_n, w_out, g, wq_t):
    n, d = x.shape
    tm = PEER_TOKENS
    aw, lw = attn_n.shape[1], lru_n.shape[1]
    qd = wq_t.shape[0]
    row = lambda i: (i, 0)
    const = lambda i: (0, 0)
    tile = lambda i: (i, 0, 0)
    return pl.pallas_call(
        _out_proj_kernel,
        grid=(n // tm,),
        in_specs=[pl.BlockSpec((tm, d), row), pl.BlockSpec((tm, aw), row), pl.BlockSpec((tm, lw), row),
                  pl.BlockSpec(w_out.shape, const), pl.BlockSpec((1, d), const), pl.BlockSpec(wq_t.shape, const)],
        out_specs=[pl.BlockSpec((tm, d), row), pl.BlockSpec((1, d, tm), tile), pl.BlockSpec((1, qd, tm), tile)],
        out_shape=[jax.ShapeDtypeStruct((n, d), F32), jax.ShapeDtypeStruct((n // tm, d, tm), BF16),
                   jax.ShapeDtypeStruct((n // tm, qd, tm), BF16)],
        compiler_params=_cparams("parallel"),
        name="out_proj",
    )(x, attn_n, lru_n, w_out, g, wq_t)


def _peer_select(qt_ref, keys_ref, s_ref, a_ref, e1_ref, ell_ref, rank_ref, e2_ref):
    nhp = keys_ref.shape[0]
    nh = nhp // 2
    t = qt_ref.shape[2]
    nch = t // LANE
    dk = keys_ref.shape[2]

    def scores(hp, _):
        q = qt_ref[0, pl.ds(pl.multiple_of(hp * dk, dk), dk), :]
        s_ref[hp] = jnp.dot(keys_ref[hp], q, preferred_element_type=F32)
        return 0

    lax.fori_loop(0, nhp, scores, 0)

    def topk(h, _):
        for half in range(2):
            hp = 2 * h + half
            for c in range(nch):
                cs = slice(c * LANE, (c + 1) * LANE)
                s0 = s_ref[hp, :, cs]
                s = s0
                rank = jnp.full(s0.shape, NOT_TOP, F32)
                for k in range(PEER_TOPK):
                    m = jnp.max(s, axis=0, keepdims=True)
                    a_ref[hp, k:k + 1, cs] = m
                    if k == 0:
                        e = jnp.exp(s0 - m)
                    hit = s == m
                    rank = jnp.where(hit, float(k), rank)
                    s = jnp.where(hit, NEG_INF, s)
                if half == 0:
                    e1_ref[h, :, cs] = e
                    ell_ref[h, :, cs] = rank
                else:
                    e2_ref[h, :, cs] = e.astype(BF16)
                    rank_ref[h, :, cs] = rank.astype(BF16)
        return 0

    lax.fori_loop(0, nh, topk, 0)

    rows8 = lax.broadcasted_iota(jnp.int32, (SUBLANE, LANE), 0)

    def cands(h, _):
        for c in range(nch):
            cs = slice(c * LANE, (c + 1) * LANE)
            a = a_ref[2 * h, :, cs]
            b = a_ref[2 * h + 1, :, cs]
            a_lo, b_lo = a[0:SUBLANE], b[0:SUBLANE]
            blocks = [a[0:1] + b_lo, a[0:1] + b[SUBLANE:], a[SUBLANE:] + b[0:1]]
            blocks += [a[i:i + 1] + b_lo for i in (1, 2, 3)]
            blocks += [jnp.where(rows8 >= 4, a_lo + b[j:j + 1], NEG_INF) for j in (0, 1, 2)]
            top = a[0:1] + b[0:1]
            z = jnp.zeros((1, LANE), F32)
            m = top
            for k in range(PEER_TOPK):
                m = blocks[0]
                for blk in blocks[1:]:
                    m = jnp.maximum(m, blk)
                m = jnp.max(m, axis=0, keepdims=True)
                z = z + jnp.exp(m - top)
                if k + 1 < PEER_TOPK:
                    blocks = [jnp.where(blk == m, NEG_INF, blk) for blk in blocks]
            tau = m
            e1_ref[h, :, cs] = e1_ref[h, :, cs] * (1.0 / z)
            rank1 = ell_ref[h, :, cs]
            ell = jnp.zeros(rank1.shape, F32)
            for r in range(PEER_TOPK):
                cnt = jnp.sum(jnp.where(a[r:r + 1] + b >= tau, 1.0, 0.0), axis=0, keepdims=True)
                ell = jnp.where(rank1 == float(r), cnt, ell)
            ell_ref[h, :, cs] = ell
        return 0

    lax.fori_loop(0, nh, cands, 0)


def _peer_kernel(h1_ref, xnt_ref, qt_ref, keys_ref, u_ref, vt_ref, o_ref,
                 s_ref, a_ref, e1_ref, ell_ref, rank_ref, e2_ref, p_ref, acc_ref):
    eb = pl.program_id(1)
    nsub = u_ref.shape[0]
    sub = u_ref.shape[1]
    nk = keys_ref.shape[1]
    nh = keys_ref.shape[0] // 2
    t = xnt_ref.shape[2]
    nch = t // LANE
    per_sub = sub // nk
    assert SUBLANE % per_sub == 0 and nsub % (SUBLANE // per_sub) == 0
    ngrp = nk // BF16_ROWS

    @pl.when(eb == 0)
    def _():
        _peer_select(qt_ref, keys_ref, s_ref, a_ref, e1_ref, ell_ref, rank_ref, e2_ref)
        acc_ref[...] = jnp.zeros_like(acc_ref)

    xnt = xnt_ref[0]
    acts = {}

    def expert_acts(sb):
        acts[sb] = jnp.dot(u_ref[sb], xnt, preferred_element_type=F32)

    def gate(sb):
        act = acts.pop(sb)
        i0 = pl.multiple_of((eb * nsub + sb) * per_sub // SUBLANE * SUBLANE, SUBLANE)
        r0 = sb * per_sub % SUBLANE
        for c in range(nch):
            cs = slice(c * LANE, (c + 1) * LANE)
            ell8 = [ell_ref[h, pl.ds(i0, SUBLANE), cs] for h in range(nh)]
            e18 = [e1_ref[h, pl.ds(i0, SUBLANE), cs] for h in range(nh)]
            for ii in range(per_sub):
                r = r0 + ii
                w = [jnp.zeros((BF16_ROWS, LANE), BF16) for _ in range(ngrp)]
                for h in range(nh):
                    ell = jnp.broadcast_to(ell8[h][r:r + 1], (BF16_ROWS, LANE)).astype(BF16)
                    e1 = jnp.broadcast_to(e18[h][r:r + 1], (BF16_ROWS, LANE)).astype(BF16)
                    for g in range(ngrp):
                        gs = slice(g * BF16_ROWS, (g + 1) * BF16_ROWS)
                        hit = rank_ref[h, gs, cs] < ell
                        w[g] = w[g] + jnp.where(hit, e2_ref[h, gs, cs] * e1, jnp.zeros_like(e1))
                ga = _gelu_tanh(act[ii * nk:(ii + 1) * nk, cs]).astype(BF16)
                for g in range(ngrp):
                    gs = slice(g * BF16_ROWS, (g + 1) * BF16_ROWS)
                    p_ref[sb, ii * nk + g * BF16_ROWS:ii * nk + (g + 1) * BF16_ROWS, cs] = w[g] * ga[gs]

    def expert_out(sb):
        acc_ref[...] += jnp.dot(vt_ref[sb], p_ref[sb], preferred_element_type=F32)

    expert_acts(0)
    if nsub > 1:
        expert_acts(1)
    for sb in range(nsub):
        gate(sb)
        expert_out(sb)
        if sb + 2 < nsub:
            expert_acts(sb + 2)

    @pl.when(eb == pl.num_programs(1) - 1)
    def _():
        o_ref[...] = h1_ref[...] + acc_ref[...].T


def _peer(h1, xn_t, q_t, keys, u3, vt3):
    n, d = h1.shape
    nt, _, t = xn_t.shape
    nblk, sub, _ = u3.shape
    nsub = PEER_SUBS_PER_STEP
    nhp, nk, dk = keys.shape
    tile = lambda i, e: (i, 0, 0)
    return pl.pallas_call(
        _peer_kernel,
        grid=(nt, nblk // nsub),
        in_specs=[pl.BlockSpec((t, d), lambda i, e: (i, 0)),
                  pl.BlockSpec((1, d, t), tile), pl.BlockSpec((1, q_t.shape[1], t), tile),
                  pl.BlockSpec(keys.shape, lambda i, e: (0, 0, 0)),
                  pl.BlockSpec((nsub, sub, d), lambda i, e: (e, 0, 0)),
                  pl.BlockSpec((nsub, d, sub), lambda i, e: (e, 0, 0))],
        out_specs=pl.BlockSpec((t, d), lambda i, e: (i, 0)),
        out_shape=jax.ShapeDtypeStruct((n, d), F32),
        scratch_shapes=[pltpu.VMEM((nhp, nk, t), F32), pltpu.VMEM((nhp, PEER_TOPK, t), F32),
                        pltpu.VMEM((nhp // 2, nk, t), F32), pltpu.VMEM((nhp // 2, nk, t), F32),
                        pltpu.VMEM((nhp // 2, nk, t), BF16), pltpu.VMEM((nhp // 2, nk, t), BF16),
                        pltpu.VMEM((nsub, sub, t), BF16), pltpu.VMEM((d, t), F32)],
        compiler_params=_cparams("parallel", "arbitrary"),
        name="peer",
    )(h1, xn_t, q_t, keys, u3, vt3)


def _final_kernel(h_ref, p_ref, gp_ref, wg_ref, wp_ref, gf_ref, y_ref):
    h = h_ref[...]
    gate = jax.nn.sigmoid(jnp.dot(_rms(h, gp_ref[...]).astype(BF16), wg_ref[...], preferred_element_type=F32))
    h = h + gate * jnp.dot(p_ref[...].astype(BF16), wp_ref[...], preferred_element_type=F32)
    y_ref[...] = _rms(h, gf_ref[...])


def _final(h2, p, g_ple, w_gate, w_proj, g_final):
    n, d = h2.shape
    tm = ROW_TILE
    pd = p.shape[1]
    row = lambda i: (i, 0)
    const = lambda i: (0, 0)
    return pl.pallas_call(
        _final_kernel,
        grid=(n // tm,),
        in_specs=[pl.BlockSpec((tm, d), row), pl.BlockSpec((tm, pd), row), pl.BlockSpec((1, d), const),
                  pl.BlockSpec(w_gate.shape, const), pl.BlockSpec(w_proj.shape, const), pl.BlockSpec((1, d), const)],
        out_specs=pl.BlockSpec((tm, d), row),
        out_shape=jax.ShapeDtypeStruct((n, d), F32),
        compiler_params=_cparams("parallel"),
        name="final",
    )(h2, p, g_ple, w_gate, w_proj, g_final)


def _rot_half_cols(w):
    d, width = w.shape
    w4 = w.reshape(d, width // HEAD_DIM, 2, HEAD_DIM // 2)
    return jnp.concatenate([-w4[:, :, 1:2], w4[:, :, 0:1]], axis=2).reshape(d, width)


def _block_diag(w):
    nb, bi, bj = w.shape
    eye = jnp.eye(nb, dtype=w.dtype)
    return (eye[:, None, :, None] * w[:, :, None, :]).reshape(nb * bi, nb * bj)


def _rope_tables(seq):
    half = HEAD_DIM // 2
    inv_freq = ROPE_THETA ** (-jnp.arange(half, dtype=F32) / half)
    ang = jnp.arange(seq, dtype=F32)[:, None] * inv_freq[None, :]
    reps = LANE // half
    return jnp.tile(jnp.cos(ang), (1, reps)), jnp.tile(jnp.sin(ang), (1, reps))


def _encoder(x, p, wts):
    b, s, d = x.shape
    n = b * s
    aw, lw = wts["aw"], wts["lw"]
    cos, sin = _rope_tables(s)
    x2 = x.reshape(n, d)
    q, k2, v2, xr, gr = _in_proj(x2, wts["mix_g"], wts["w_in"], cos, sin, s, aw, lw)
    attn_n = _attention(q, k2, v2, wts["sink"], wts["attn_g"], s)
    hf = _lru_fwd(xr, wts["conv_w"], wts["conv_b"], wts["wg"][0], wts["bg"][0], wts["lam"][0], b, s)
    lru_n = _lru_bwd(xr, wts["conv_w"], wts["conv_b"], wts["wg"][1], wts["bg"][1], wts["lam"][1],
                     gr, hf, wts["lru_g"], b, s)
    h1, xn_t, q_t = _out_proj(x2, attn_n, lru_n, wts["w_out"], wts["ffn_g"], wts["wq_t"])
    h2 = _peer(h1, xn_t, q_t, wts["keys"], wts["u3"], wts["vt3"])
    y = _final(h2, p.reshape(n, -1), wts["ple_g"], wts["w_gate"], wts["w_proj"], wts["final_g"])
    return y.reshape(b, s, d)


def kernel(x_prompt, x_sample, p_prompt, p_sample, mix_norm_g, w_in, attn_sink, conv_w, conv_b, lru_wa, lru_ba, lru_wx, lru_bx, lru_lambda, attn_out_norm_g, lru_out_norm_g, w_out, ffn_norm_g, peer_wq, peer_keys, peer_u, peer_v, ple_norm_g, ple_w_gate, ple_w_proj, final_norm_g):
    depth = w_in.shape[0]
    assert depth == 1, "single-layer encoder"
    l = 0
    d = w_in.shape[1]
    aw = N_Q_HEADS * HEAD_DIM
    kw = N_KV_HEADS * HEAD_DIM
    lw = conv_w.shape[2]
    assert kw == LANE and aw % LANE == 0
    o1, o2, o3, o4 = aw, aw + kw, aw + 2 * kw, aw + 2 * kw + lw
    w = w_in[l]
    wq_, wk_, wv_, wx_, wg_ = w[:, :o1], w[:, o1:o2], w[:, o2:o3], w[:, o3:o4], w[:, o4:]
    w_ext = jnp.concatenate([wq_, _rot_half_cols(wq_), wk_, _rot_half_cols(wk_), wv_, wx_, wg_], axis=1).astype(BF16)
    nh, _, nk, dk = peer_keys.shape[1:]
    n_exp = peer_u.shape[1]
    assert n_exp == nk * nk and nk == LANE and dk == LANE
    nblk = n_exp // PEER_SUB
    wts = dict(
        aw=aw, lw=lw,
        mix_g=mix_norm_g[l][None], w_in=w_ext, sink=attn_sink[l],
        attn_g=attn_out_norm_g[l][None], lru_g=lru_out_norm_g[l][None],
        conv_w=conv_w[l], conv_b=conv_b[l][None],
        wg=[jnp.concatenate([_block_diag(lru_wa[l, k]), _block_diag(lru_wx[l, k])], axis=1).astype(BF16) for k in range(2)],
        bg=[jnp.concatenate([lru_ba[l, k], lru_bx[l, k]])[None] for k in range(2)],
        lam=[lru_lambda[l, k][None] for k in range(2)],
        w_out=w_out[l].astype(BF16), ffn_g=ffn_norm_g[l][None],
        wq_t=peer_wq[l].T.astype(BF16),
        keys=peer_keys[l].reshape(nh * 2, nk, dk).astype(BF16),
        u3=peer_u[l].astype(BF16).reshape(nblk, PEER_SUB, d),
        vt3=peer_v[l].astype(BF16).reshape(nblk, PEER_SUB, d).transpose(0, 2, 1),
        ple_g=ple_norm_g[l][None], w_gate=ple_w_gate[l].astype(BF16), w_proj=ple_w_proj[l].astype(BF16),
        final_g=final_norm_g[None],
    )
    y_prompt = _encoder(x_prompt, p_prompt[l], wts)
    y_sample = _encoder(x_sample, p_sample[l], wts)
    return (y_prompt, y_sample)
```

```python
import functools
import math

import jax
import jax.numpy as jnp
import numpy as np
from jax import lax
from jax.experimental import pallas as pl
from jax.experimental.pallas import tpu as pltpu

F32 = jnp.float32
BF16 = jnp.bfloat16

EPS = 1e-6
N_Q_HEADS = 8
N_KV_HEADS = 2
HEAD_DIM = 64
WINDOW = 128
ROPE_THETA = 10000.0
LRU_C = 8.0
CONV_LEFT = 2
PEER_TOPK = 16

LANE = 128
SUBLANE = 8
BF16_ROWS = 16
ROW_TILE = 512
PEER_TOKENS = 512
PEER_SUB = 512
PEER_SUBS_PER_STEP = 4
VMEM_LIMIT = 56 * 1024 * 1024
NEG_INF = float("-inf")


def _cparams(*sem):
    return pltpu.CompilerParams(dimension_semantics=sem, vmem_limit_bytes=VMEM_LIMIT)


def _rms(x, g):
    return x * lax.rsqrt(jnp.mean(x * x, axis=-1, keepdims=True) + EPS) * g


_GELU_K0 = -2.0 * math.sqrt(2.0 / math.pi) * math.log2(math.e)
_GELU_K1 = _GELU_K0 * 0.044715


def _gelu_tanh(x):
    return x / (1.0 + jnp.exp2(x * (_GELU_K0 + _GELU_K1 * (x * x))))


def _in_proj_kernel(x_ref, g_ref, w_ref, cos_ref, sin_ref, q_ref, k_ref, v_ref, xr_ref, gr_ref):
    aw, kw, lw = q_ref.shape[1], LANE, xr_ref.shape[1]
    xn = _rms(x_ref[...], g_ref[...]).astype(BF16)
    z = jnp.dot(xn, w_ref[...], preferred_element_type=F32)
    c = cos_ref[...]
    s = sin_ref[...]
    reps = aw // LANE
    cq = jnp.concatenate([c] * reps, axis=1)
    sq = jnp.concatenate([s] * reps, axis=1)
    o = 0
    q = (z[:, o:o + aw] * cq + z[:, o + aw:o + 2 * aw] * sq) * (HEAD_DIM ** -0.5)
    o += 2 * aw
    k = z[:, o:o + kw] * c + z[:, o + kw:o + 2 * kw] * s
    o += 2 * kw
    v = z[:, o:o + kw]
    o += kw
    q_ref[...] = q.astype(BF16)
    k_ref[...] = k.astype(BF16)
    v_ref[...] = v.astype(BF16)
    xr_ref[...] = z[:, o:o + lw]
    gr_ref[...] = z[:, o + lw:o + 2 * lw]


def _in_proj(x, g, w_ext, cos, sin, seq, aw, lw):
    n, d = x.shape
    tm = ROW_TILE
    per_seq = seq // tm
    row = lambda i: (i, 0)
    const = lambda i: (0, 0)
    pos = lambda i: (i % per_seq, 0)
    return pl.pallas_call(
        _in_proj_kernel,
        grid=(n // tm,),
        in_specs=[pl.BlockSpec((tm, d), row), pl.BlockSpec((1, d), const),
                  pl.BlockSpec(w_ext.shape, const),
                  pl.BlockSpec((tm, LANE), pos), pl.BlockSpec((tm, LANE), pos)],
        out_specs=[pl.BlockSpec((tm, aw), row), pl.BlockSpec((tm, LANE), row),
                   pl.BlockSpec((tm, LANE), row), pl.BlockSpec((tm, lw), row),
                   pl.BlockSpec((tm, lw), row)],
        out_shape=[jax.ShapeDtypeStruct((n, aw), BF16), jax.ShapeDtypeStruct((n, LANE), BF16),
                   jax.ShapeDtypeStruct((n, LANE), BF16), jax.ShapeDtypeStruct((n, lw), F32),
                   jax.ShapeDtypeStruct((n, lw), F32)],
        compiler_params=_cparams("parallel"),
        name="in_proj",
    )(x, g, w_ext, cos, sin)


def _attn_kernel(sink_ref, q_ref, kp_ref, ks_ref, kn_ref, vp_ref, vs_ref, vn_ref, g_ref, o_ref, *, seq):
    qb = q_ref.shape[0]
    blk = WINDOW
    p0 = (pl.program_id(0) % (seq // qb)) * qb
    kf = jnp.concatenate([kp_ref[...], ks_ref[...], kn_ref[...]], axis=0)
    vf = jnp.concatenate([vp_ref[...], vs_ref[...], vn_ref[...]], axis=0)
    lane = lax.broadcasted_iota(jnp.int32, (blk, LANE), 1)
    row = lax.broadcasted_iota(jnp.int32, (blk, 3 * blk), 0)
    col = lax.broadcasted_iota(jnp.int32, (blk, 3 * blk), 1)
    rel = col - row
    band = (rel >= 0) & (rel <= 2 * WINDOW)
    grp = N_Q_HEADS // N_KV_HEADS
    for r in range(qb // blk):
        kpos = p0 + (r - 1) * blk + col
        valid = band & (kpos >= 0) & (kpos < seq)
        q4 = [q_ref[r * blk:(r + 1) * blk, m * LANE:(m + 1) * LANE] for m in range(grp)]
        zero = jnp.zeros_like(q4[0])
        stacked = ([jnp.where(lane < HEAD_DIM, qs, zero) for qs in q4]
                   + [jnp.where(lane >= HEAD_DIM, qs, zero) for qs in q4])
        qs = jnp.concatenate(stacked, axis=0)
        kk = kf[r * blk:(r + 3) * blk]
        vv = vf[r * blk:(r + 3) * blk]
        s_all = lax.dot_general(qs, kk, (((1,), (1,)), ((), ())), preferred_element_type=F32)
        probs = []
        for h in range(N_Q_HEADS):
            s = jnp.where(valid, s_all[h * blk:(h + 1) * blk], -1e30)
            sk = sink_ref[h]
            mx = jnp.maximum(jnp.max(s, axis=1, keepdims=True), sk)
            e = jnp.exp(s - mx)
            den = jnp.sum(e, axis=1, keepdims=True) + jnp.exp(sk - mx)
            probs.append((e * (1.0 / den)).astype(BF16))
        o_all = jnp.dot(jnp.concatenate(probs, axis=0), vv, preferred_element_type=F32)
        attn = jnp.concatenate(
            [jnp.where(lane < HEAD_DIM, o_all[m * blk:(m + 1) * blk], o_all[(grp + m) * blk:(grp + m + 1) * blk])
             for m in range(grp)], axis=1)
        o_ref[r * blk:(r + 1) * blk, :] = _rms(attn, g_ref[...]).astype(BF16)


def _attention(q, k2, v2, sink, g, seq):
    n, aw = q.shape
    qb = ROW_TILE
    per = qb // WINDOW
    nblk = n // WINDOW
    row = lambda i: (i, 0)
    prev = lambda i: (jnp.maximum(i * per - 1, 0), 0)
    nxt = lambda i: (jnp.minimum((i + 1) * per, nblk - 1), 0)
    kw = k2.shape[1]
    return pl.pallas_call(
        functools.partial(_attn_kernel, seq=seq),
        grid=(n // qb,),
        in_specs=[pl.BlockSpec(memory_space=pltpu.SMEM),
                  pl.BlockSpec((qb, aw), row),
                  pl.BlockSpec((WINDOW, kw), prev), pl.BlockSpec((qb, kw), row), pl.BlockSpec((WINDOW, kw), nxt),
                  pl.BlockSpec((WINDOW, kw), prev), pl.BlockSpec((qb, kw), row), pl.BlockSpec((WINDOW, kw), nxt),
                  pl.BlockSpec((1, aw), lambda i: (0, 0))],
        out_specs=pl.BlockSpec((qb, aw), row),
        out_shape=jax.ShapeDtypeStruct((n, aw), BF16),
        compiler_params=_cparams("parallel"),
        name="attention",
    )(sink, q, k2, k2, k2, v2, v2, v2, g)


def _lru_gates(x_ref, xp_ref, xn_ref, cw_ref, cb_ref, wg_ref, bg_ref, lam_ref, ext_ref, a_ref, u_ref,
               has_prev, has_next):
    tb, lw = x_ref.shape
    x = x_ref[...]
    ext_ref[0:SUBLANE, :] = jnp.where(has_prev, xp_ref[...], 0.0)
    ext_ref[SUBLANE:SUBLANE + tb, :] = x
    ext_ref[SUBLANE + tb:2 * SUBLANE + tb, :] = jnp.where(has_next, xn_ref[...], 0.0)
    cw = cw_ref[...]
    xc = cb_ref[...] + cw[2:3] * x
    for j in (0, 1, 3):
        off = SUBLANE + j - CONV_LEFT
        xc = xc + cw[j:j + 1] * ext_ref[off:off + tb, :]
    gates = jnp.dot(xc.astype(BF16), wg_ref[...], preferred_element_type=F32) + bg_ref[...]
    gate_r = jax.nn.sigmoid(gates[:, :lw])
    gate_i = jax.nn.sigmoid(gates[:, lw:])
    lam = lam_ref[...]
    log_sig = jnp.minimum(lam, 0.0) - jnp.log1p(jnp.exp(-jnp.abs(lam)))
    log_a = LRU_C * gate_r * log_sig
    a = jnp.exp(log_a)
    a_ref[...] = a
    u_ref[...] = jnp.sqrt(-jnp.tanh(log_a) * (a * a + 1.0)) * (gate_i * xc)


def _lru_fwd_kernel(x_ref, xp_ref, xn_ref, cw_ref, cb_ref, wg_ref, bg_ref, lam_ref, h_ref,
                    ext_ref, a_ref, u_ref, carry_ref):
    j = pl.program_id(1)
    nt = pl.num_programs(1)
    tb = x_ref.shape[0]

    @pl.when(j == 0)
    def _():
        carry_ref[...] = jnp.zeros_like(carry_ref)

    _lru_gates(x_ref, xp_ref, xn_ref, cw_ref, cb_ref, wg_ref, bg_ref, lam_ref, ext_ref, a_ref, u_ref,
               j > 0, j < nt - 1)

    def body(t, h):
        h = a_ref[pl.ds(t, 1), :] * h + u_ref[pl.ds(t, 1), :]
        h_ref[pl.ds(t, 1), :] = h
        return h

    carry_ref[...] = lax.fori_loop(0, tb, body, carry_ref[...], unroll=8)


def _lru_bwd_kernel(x_ref, xp_ref, xn_ref, cw_ref, cb_ref, wg_ref, bg_ref, lam_ref, gr_ref, hf_ref, g_ref,
                    o_ref, ext_ref, a_ref, u_ref, hb_ref, carry_ref):
    j = pl.program_id(1)
    nt = pl.num_programs(1)
    tb = x_ref.shape[0]

    @pl.when(j == 0)
    def _():
        carry_ref[...] = jnp.zeros_like(carry_ref)

    _lru_gates(x_ref, xp_ref, xn_ref, cw_ref, cb_ref, wg_ref, bg_ref, lam_ref, ext_ref, a_ref, u_ref,
               j < nt - 1, j > 0)

    def body(i, h):
        t = tb - 1 - i
        h = a_ref[pl.ds(t, 1), :] * h + u_ref[pl.ds(t, 1), :]
        hb_ref[pl.ds(t, 1), :] = h
        return h

    carry_ref[...] = lax.fori_loop(0, tb, body, carry_ref[...], unroll=8)
    lru = jax.nn.gelu(gr_ref[...]) * (hf_ref[...] + hb_ref[...])
    o_ref[...] = _rms(lru, g_ref[...]).astype(BF16)


def _lru_specs(batch, seq, lw, tb, reverse):
    nt = seq // tb
    per8 = tb // SUBLANE
    n8 = batch * seq // SUBLANE

    def blk(b, j):
        return b * nt + (nt - 1 - j if reverse else j)

    main = lambda b, j: (blk(b, j), 0)
    prev = lambda b, j: (jnp.maximum(blk(b, j) * per8 - 1, 0), 0)
    nxt = lambda b, j: (jnp.minimum((blk(b, j) + 1) * per8, n8 - 1), 0)
    const = lambda b, j: (0, 0)
    specs = [pl.BlockSpec((tb, lw), main), pl.BlockSpec((SUBLANE, lw), prev), pl.BlockSpec((SUBLANE, lw), nxt),
             pl.BlockSpec((4, lw), const), pl.BlockSpec((1, lw), const),
             pl.BlockSpec((lw, 2 * lw), const), pl.BlockSpec((1, 2 * lw), const), pl.BlockSpec((1, lw), const)]
    return specs, main, const, nt


def _lru_fwd(xr, cw, cb, wg, bg, lam, batch, seq):
    n, lw = xr.shape
    tb = ROW_TILE
    specs, main, _, nt = _lru_specs(batch, seq, lw, tb, reverse=False)
    return pl.pallas_call(
        _lru_fwd_kernel,
        grid=(batch, nt),
        in_specs=specs,
        out_specs=pl.BlockSpec((tb, lw), main),
        out_shape=jax.ShapeDtypeStruct((n, lw), F32),
        scratch_shapes=[pltpu.VMEM((tb + 2 * SUBLANE, lw), F32), pltpu.VMEM((tb, lw), F32),
                        pltpu.VMEM((tb, lw), F32), pltpu.VMEM((1, lw), F32)],
        compiler_params=_cparams("parallel", "arbitrary"),
        name="lru_fwd",
    )(xr, xr, xr, cw, cb, wg, bg, lam)


def _lru_bwd(xr, cw, cb, wg, bg, lam, gr, hf, g, batch, seq):
    n, lw = xr.shape
    tb = ROW_TILE
    specs, main, const, nt = _lru_specs(batch, seq, lw, tb, reverse=True)
    specs = specs + [pl.BlockSpec((tb, lw), main), pl.BlockSpec((tb, lw), main), pl.BlockSpec((1, lw), const)]
    return pl.pallas_call(
        _lru_bwd_kernel,
        grid=(batch, nt),
        in_specs=specs,
        out_specs=pl.BlockSpec((tb, lw), main),
        out_shape=jax.ShapeDtypeStruct((n, lw), BF16),
        scratch_shapes=[pltpu.VMEM((tb + 2 * SUBLANE, lw), F32), pltpu.VMEM((tb, lw), F32),
                        pltpu.VMEM((tb, lw), F32), pltpu.VMEM((tb, lw), F32), pltpu.VMEM((1, lw), F32)],
        compiler_params=_cparams("parallel", "arbitrary"),
        name="lru_bwd",
    )(xr, xr, xr, cw, cb, wg, bg, lam, gr, hf, g)


def _out_proj_kernel(x_ref, at_ref, lr_ref, wo_ref, g_ref, wqt_ref, h1_ref, xnt_ref, qt_ref):
    aw = at_ref.shape[1]
    h1 = (x_ref[...] + jnp.dot(at_ref[...], wo_ref[0:aw, :], preferred_element_type=F32)
          + jnp.dot(lr_ref[...], wo_ref[aw:, :], preferred_element_type=F32))
    h1_ref[...] = h1
    xnt = _rms(h1, g_ref[...]).T.astype(BF16)
    xnt_ref[0] = xnt
    qt_ref[0] = jnp.dot(wqt_ref[...], xnt, preferred_element_type=F32).astype(BF16)


def _out_proj(x, attn_n, lru_n, w_out, g, wq_t):
    n, d = x.shape
    tm = PEER_TOKENS
    aw, lw = attn_n.shape[1], lru_n.shape[1]
    qd = wq_t.shape[0]
    row = lambda i: (i, 0)
    const = lambda i: (0, 0)
    tile = lambda i: (i, 0, 0)
    return pl.pallas_call(
        _out_proj_kernel,
        grid=(n // tm,),
        in_specs=[pl.BlockSpec((tm, d), row), pl.BlockSpec((tm, aw), row), pl.BlockSpec((tm, lw), row),
                  pl.BlockSpec(w_out.shape, const), pl.BlockSpec((1, d), const), pl.BlockSpec(wq_t.shape, const)],
        out_specs=[pl.BlockSpec((tm, d), row), pl.BlockSpec((1, d, tm), tile), pl.BlockSpec((1, qd, tm), tile)],
        out_shape=[jax.ShapeDtypeStruct((n, d), F32), jax.ShapeDtypeStruct((n // tm, d, tm), BF16),
                   jax.ShapeDtypeStruct((n // tm, qd, tm), BF16)],
        compiler_params=_cparams("parallel"),
        name="out_proj",
    )(x, attn_n, lru_n, w_out, g, wq_t)


def _batcher_network(lo, hi):
    def merge(lo, hi, r):
        step = r * 2
        if step < hi - lo:
            yield from merge(lo, hi, step)
            yield from merge(lo + r, hi, step)
            yield from [(i, i + r) for i in range(lo + r, hi - r, step)]
        else:
            yield (lo, lo + r)

    if hi - lo >= 1:
        mid = lo + (hi - lo) // 2
        yield from _batcher_network(lo, mid)
        yield from _batcher_network(mid + 1, hi)
        yield from merge(lo, hi, 1)


def _exchange(x, i, j):
    x[i], x[j] = jnp.maximum(x[i], x[j]), jnp.minimum(x[i], x[j])


def _sorted_top16(groups):
    n = PEER_TOPK
    assert len(groups) == n
    x = list(groups)
    for i, j in _batcher_network(0, n - 1):
        _exchange(x, i, j)
    shift = SUBLANE // 2
    while shift:
        rolled = [pltpu.roll(v, shift, 0) for v in x]
        x = [jnp.maximum(x[k], rolled[n - 1 - k]) for k in range(n)]
        d = n // 2
        while d:
            for k in range(n):
                if not k & d:
                    _exchange(x, k, k + d)
            d //= 2
        shift //= 2
    return x


def _prefix_count(test, rows):
    w = jnp.where
    g1 = test(rows[7])
    g2 = test(w(g1, rows[11], rows[3]))
    g3 = test(w(g1, w(g2, rows[13], rows[9]), w(g2, rows[5], rows[1])))
    g4 = test(w(g1, w(g2, w(g3, rows[14], rows[12]), w(g3, rows[10], rows[8])),
                w(g2, w(g3, rows[6], rows[4]), w(g3, rows[2], rows[0]))))
    cnt = w(g1, 8.0, 0.0) + w(g2, 4.0, 0.0) + w(g3, 2.0, 0.0) + w(g4, 1.0, 0.0)
    return w(test(rows[15]), 16.0, cnt)


def _peer_select(qt_ref, keys_ref, s_ref, a_ref, e1_ref, ell_ref, rank_ref, e2_ref):
    nhp = keys_ref.shape[0]
    nh = nhp // 2
    t = qt_ref.shape[2]
    nch = t // LANE
    dk = keys_ref.shape[2]

    def scores(hp, _):
        q = qt_ref[0, pl.ds(pl.multiple_of(hp * dk, dk), dk), :]
        s_ref[hp] = jnp.dot(keys_ref[hp], q, preferred_element_type=F32)
        return 0

    lax.fori_loop(0, nhp, scores, 0)

    nk = keys_ref.shape[1]
    ngrp = nk // SUBLANE

    def topk(h, _):
        for half in range(2):
            hp = 2 * h + half
            for c in range(nch):
                cs = slice(c * LANE, (c + 1) * LANE)
                s0 = s_ref[hp, :, cs]
                groups = [s0[g * SUBLANE:(g + 1) * SUBLANE] for g in range(ngrp)]
                top = _sorted_top16(groups)
                for k in range(PEER_TOPK):
                    a_ref[hp, k:k + 1, cs] = top[k][0:1]
                e = jnp.exp(s0 - top[0][0:1])
                if half == 0:
                    e1_ref[h, :, cs] = e
                else:
                    e2_ref[h, :, cs] = e.astype(BF16)
                    rank = [_prefix_count(lambda thr, v=v: thr > v, top) for v in groups]
                    rank_ref[h, :, cs] = jnp.concatenate(rank, axis=0).astype(BF16)
        return 0

    lax.fori_loop(0, nh, topk, 0)

    rows8 = lax.broadcasted_iota(jnp.int32, (SUBLANE, LANE), 0)

    def cands(h, _):
        for c in range(nch):
            cs = slice(c * LANE, (c + 1) * LANE)
            a = a_ref[2 * h, :, cs]
            b = a_ref[2 * h + 1, :, cs]
            a_lo, b_lo = a[0:SUBLANE], b[0:SUBLANE]
            blocks = [a[0:1] + b_lo, a[0:1] + b[SUBLANE:], a[SUBLANE:] + b[0:1]]
            blocks += [a[i:i + 1] + b_lo for i in (1, 2, 3)]
            blocks += [jnp.where(rows8 >= 4, a_lo + b[j:j + 1], NEG_INF) for j in (0, 1, 2)]
            top = a[0:1] + b[0:1]
            z = jnp.zeros((1, LANE), F32)
            m = top
            for k in range(PEER_TOPK):
                m = blocks[0]
                for blk in blocks[1:]:
                    m = jnp.maximum(m, blk)
                m = jnp.max(m, axis=0, keepdims=True)
                z = z + jnp.exp(m - top)
                if k + 1 < PEER_TOPK:
                    blocks = [jnp.where(blk == m, NEG_INF, blk) for blk in blocks]
            tau = jnp.broadcast_to(m, (SUBLANE, LANE))
            e1_ref[h, :, cs] = e1_ref[h, :, cs] * (1.0 / z)
            b_rows = [jnp.broadcast_to(b[r:r + 1], (SUBLANE, LANE)) for r in range(PEER_TOPK)]
            s1 = s_ref[2 * h, :, cs]
            ell = [_prefix_count(lambda thr, v=s1[g * SUBLANE:(g + 1) * SUBLANE]: v + thr >= tau, b_rows)
                   for g in range(ngrp)]
            ell_ref[h, :, cs] = jnp.concatenate(ell, axis=0)
        return 0

    lax.fori_loop(0, nh, cands, 0)


def _peer_kernel(h1_ref, xnt_ref, qt_ref, keys_ref, u_ref, vt_ref, o_ref,
                 s_ref, a_ref, e1_ref, ell_ref, rank_ref, e2_ref, p_ref, acc_ref):
    eb = pl.program_id(1)
    nsub = u_ref.shape[0]
    sub = u_ref.shape[1]
    nk = keys_ref.shape[1]
    nh = keys_ref.shape[0] // 2
    t = xnt_ref.shape[2]
    nch = t // LANE
    per_sub = sub // nk
    assert SUBLANE % per_sub == 0 and nsub % (SUBLANE // per_sub) == 0
    ngrp = nk // BF16_ROWS

    @pl.when(eb == 0)
    def _():
        _peer_select(qt_ref, keys_ref, s_ref, a_ref, e1_ref, ell_ref, rank_ref, e2_ref)
        acc_ref[...] = jnp.zeros_like(acc_ref)

    xnt = xnt_ref[0]
    acts = {}

    def expert_acts(sb):
        acts[sb] = jnp.dot(u_ref[sb], xnt, preferred_element_type=F32)

    def gate(sb):
        act = acts.pop(sb)
        i0 = pl.multiple_of((eb * nsub + sb) * per_sub // SUBLANE * SUBLANE, SUBLANE)
        r0 = sb * per_sub % SUBLANE
        for c in range(nch):
            cs = slice(c * LANE, (c + 1) * LANE)
            ell8 = [ell_ref[h, pl.ds(i0, SUBLANE), cs] for h in range(nh)]
            e18 = [e1_ref[h, pl.ds(i0, SUBLANE), cs] for h in range(nh)]
            for ii in range(per_sub):
                r = r0 + ii
                w = [None] * ngrp
                for h in range(nh):
                    ell = jnp.broadcast_to(ell8[h][r:r + 1], (BF16_ROWS, LANE)).astype(BF16)
                    e1 = jnp.broadcast_to(e18[h][r:r + 1], (BF16_ROWS, LANE)).astype(BF16)
                    for g in range(ngrp):
                        gs = slice(g * BF16_ROWS, (g + 1) * BF16_ROWS)
                        hit = rank_ref[h, gs, cs] < ell
                        term = jnp.where(hit, e2_ref[h, gs, cs] * e1, jnp.zeros_like(e1))
                        w[g] = term if h == 0 else w[g] + term
                ga = _gelu_tanh(act[ii * nk:(ii + 1) * nk, cs]).astype(BF16)
                for g in range(ngrp):
                    gs = slice(g * BF16_ROWS, (g + 1) * BF16_ROWS)
                    p_ref[sb, ii * nk + g * BF16_ROWS:ii * nk + (g + 1) * BF16_ROWS, cs] = w[g] * ga[gs]

    def expert_out(sb):
        acc_ref[...] += jnp.dot(vt_ref[sb], p_ref[sb], preferred_element_type=F32)

    expert_acts(0)
    if nsub > 1:
        expert_acts(1)
    for sb in range(nsub):
        gate(sb)
        expert_out(sb)
        if sb + 2 < nsub:
            expert_acts(sb + 2)

    @pl.when(eb == pl.num_programs(1) - 1)
    def _():
        o_ref[...] = h1_ref[...] + acc_ref[...].T


def _peer(h1, xn_t, q_t, keys, u3, vt3):
    n, d = h1.shape
    nt, _, t = xn_t.shape
    nblk, sub, _ = u3.shape
    nsub = PEER_SUBS_PER_STEP
    nhp, nk, dk = keys.shape
    tile = lambda i, e: (i, 0, 0)
    return pl.pallas_call(
        _peer_kernel,
        grid=(nt, nblk // nsub),
        in_specs=[pl.BlockSpec((t, d), lambda i, e: (i, 0)),
                  pl.BlockSpec((1, d, t), tile), pl.BlockSpec((1, q_t.shape[1], t), tile),
                  pl.BlockSpec(keys.shape, lambda i, e: (0, 0, 0)),
                  pl.BlockSpec((nsub, sub, d), lambda i, e: (e, 0, 0)),
                  pl.BlockSpec((nsub, d, sub), lambda i, e: (e, 0, 0))],
        out_specs=pl.BlockSpec((t, d), lambda i, e: (i, 0)),
        out_shape=jax.ShapeDtypeStruct((n, d), F32),
        scratch_shapes=[pltpu.VMEM((nhp, nk, t), F32), pltpu.VMEM((nhp, PEER_TOPK, t), F32),
                        pltpu.VMEM((nhp // 2, nk, t), F32), pltpu.VMEM((nhp // 2, nk, t), F32),
                        pltpu.VMEM((nhp // 2, nk, t), BF16), pltpu.VMEM((nhp // 2, nk, t), BF16),
                        pltpu.VMEM((nsub, sub, t), BF16), pltpu.VMEM((d, t), F32)],
        compiler_params=_cparams("parallel", "arbitrary"),
        name="peer",
    )(h1, xn_t, q_t, keys, u3, vt3)


def _final_kernel(h_ref, p_ref, gp_ref, wg_ref, wp_ref, gf_ref, y_ref):
    h = h_ref[...]
    gate = jax.nn.sigmoid(jnp.dot(_rms(h, gp_ref[...]).astype(BF16), wg_ref[...], preferred_element_type=F32))
    h = h + gate * jnp.dot(p_ref[...].astype(BF16), wp_ref[...], preferred_element_type=F32)
    y_ref[...] = _rms(h, gf_ref[...])


def _final(h2, p, g_ple, w_gate, w_proj, g_final):
    n, d = h2.shape
    tm = ROW_TILE
    pd = p.shape[1]
    row = lambda i: (i, 0)
    const = lambda i: (0, 0)
    return pl.pallas_call(
        _final_kernel,
        grid=(n // tm,),
        in_specs=[pl.BlockSpec((tm, d), row), pl.BlockSpec((tm, pd), row), pl.BlockSpec((1, d), const),
                  pl.BlockSpec(w_gate.shape, const), pl.BlockSpec(w_proj.shape, const), pl.BlockSpec((1, d), const)],
        out_specs=pl.BlockSpec((tm, d), row),
        out_shape=jax.ShapeDtypeStruct((n, d), F32),
        compiler_params=_cparams("parallel"),
        name="final",
    )(h2, p, g_ple, w_gate, w_proj, g_final)


def _rot_half_cols(w):
    d, width = w.shape
    w4 = w.reshape(d, width // HEAD_DIM, 2, HEAD_DIM // 2)
    return jnp.concatenate([-w4[:, :, 1:2], w4[:, :, 0:1]], axis=2).reshape(d, width)


def _block_diag(w):
    nb, bi, bj = w.shape
    eye = jnp.eye(nb, dtype=w.dtype)
    return (eye[:, None, :, None] * w[:, :, None, :]).reshape(nb * bi, nb * bj)


def _rope_tables(seq):
    half = HEAD_DIM // 2
    inv_freq = ROPE_THETA ** (-jnp.arange(half, dtype=F32) / half)
    ang = jnp.arange(seq, dtype=F32)[:, None] * inv_freq[None, :]
    reps = LANE // half
    return jnp.tile(jnp.cos(ang), (1, reps)), jnp.tile(jnp.sin(ang), (1, reps))


def _encoder(x, p, wts):
    b, s, d = x.shape
    n = b * s
    aw, lw = wts["aw"], wts["lw"]
    cos, sin = _rope_tables(s)
    x2 = x.reshape(n, d)
    q, k2, v2, xr, gr = _in_proj(x2, wts["mix_g"], wts["w_in"], cos, sin, s, aw, lw)
    attn_n = _attention(q, k2, v2, wts["sink"], wts["attn_g"], s)
    hf = _lru_fwd(xr, wts["conv_w"], wts["conv_b"], wts["wg"][0], wts["bg"][0], wts["lam"][0], b, s)
    lru_n = _lru_bwd(xr, wts["conv_w"], wts["conv_b"], wts["wg"][1], wts["bg"][1], wts["lam"][1],
                     gr, hf, wts["lru_g"], b, s)
    h1, xn_t, q_t = _out_proj(x2, attn_n, lru_n, wts["w_out"], wts["ffn_g"], wts["wq_t"])
    h2 = _peer(h1, xn_t, q_t, wts["keys"], wts["u3"], wts["vt3"])
    y = _final(h2, p.reshape(n, -1), wts["ple_g"], wts["w_gate"], wts["w_proj"], wts["final_g"])
    return y.reshape(b, s, d)


def kernel(x_prompt, x_sample, p_prompt, p_sample, mix_norm_g, w_in, attn_sink, conv_w, conv_b, lru_wa, lru_ba, lru_wx, lru_bx, lru_lambda, attn_out_norm_g, lru_out_norm_g, w_out, ffn_norm_g, peer_wq, peer_keys, peer_u, peer_v, ple_norm_g, ple_w_gate, ple_w_proj, final_norm_g):
    depth = w_in.shape[0]
    assert depth == 1, "single-layer encoder"
    l = 0
    d = w_in.shape[1]
    aw = N_Q_HEADS * HEAD_DIM
    kw = N_KV_HEADS * HEAD_DIM
    lw = conv_w.shape[2]
    assert kw == LANE and aw % LANE == 0
    o1, o2, o3, o4 = aw, aw + kw, aw + 2 * kw, aw + 2 * kw + lw
    w = w_in[l]
    wq_, wk_, wv_, wx_, wg_ = w[:, :o1], w[:, o1:o2], w[:, o2:o3], w[:, o3:o4], w[:, o4:]
    grp = N_Q_HEADS // N_KV_HEADS
    head_order = np.array([m + kv * grp for m in range(grp) for kv in range(N_KV_HEADS)])
    feat_order = (head_order[:, None] * HEAD_DIM + np.arange(HEAD_DIM)[None, :]).reshape(-1)
    wq_ = wq_[:, feat_order]
    w_out_l = jnp.concatenate([w_out[l][:aw][feat_order], w_out[l][aw:]], axis=0)
    w_ext = jnp.concatenate([wq_, _rot_half_cols(wq_), wk_, _rot_half_cols(wk_), wv_, wx_, wg_], axis=1).astype(BF16)
    nh, _, nk, dk = peer_keys.shape[1:]
    n_exp = peer_u.shape[1]
    assert n_exp == nk * nk and nk == LANE and dk == LANE
    nblk = n_exp // PEER_SUB
    wts = dict(
        aw=aw, lw=lw,
        mix_g=mix_norm_g[l][None], w_in=w_ext, sink=attn_sink[l],
        attn_g=attn_out_norm_g[l][feat_order][None], lru_g=lru_out_norm_g[l][None],
        conv_w=conv_w[l], conv_b=conv_b[l][None],
        wg=[jnp.concatenate([_block_diag(lru_wa[l, k]), _block_diag(lru_wx[l, k])], axis=1).astype(BF16) for k in range(2)],
        bg=[jnp.concatenate([lru_ba[l, k], lru_bx[l, k]])[None] for k in range(2)],
        lam=[lru_lambda[l, k][None] for k in range(2)],
        w_out=w_out_l.astype(BF16), ffn_g=ffn_norm_g[l][None],
        wq_t=peer_wq[l].T.astype(BF16),
        keys=peer_keys[l].reshape(nh * 2, nk, dk).astype(BF16),
        u3=peer_u[l].astype(BF16).reshape(nblk, PEER_SUB, d),
        vt3=peer_v[l].astype(BF16).reshape(nblk, PEER_SUB, d).transpose(0, 2, 1),
        ple_g=ple_norm_g[l][None], w_gate=ple_w_gate[l].astype(BF16), w_proj=ple_w_proj[l].astype(BF16),
        final_g=final_norm_g[None],
    )
    y_prompt = _encoder(x_prompt, p_prompt[l], wts)
    y_sample = _encoder(x_sample, p_sample[l], wts)
    return (y_prompt, y_sample)
```

```python
import functools
import math

import jax
import jax.numpy as jnp
import numpy as np
from jax import lax
from jax.experimental import pallas as pl
from jax.experimental.pallas import tpu as pltpu

F32 = jnp.float32
BF16 = jnp.bfloat16

EPS = 1e-6
N_Q_HEADS = 8
N_KV_HEADS = 2
HEAD_DIM = 64
WINDOW = 128
ROPE_THETA = 10000.0
LRU_C = 8.0
CONV_LEFT = 2
PEER_TOPK = 16

LANE = 128
SUBLANE = 8
GATE_DTYPE = jnp.bfloat16
GATE_ROWS = 16
GATE_KEYS = 2
ROW_TILE = 512
PEER_TOKENS = 512
PEER_SUB = 512
PEER_SUBS_PER_STEP = 4
VMEM_LIMIT = 56 * 1024 * 1024
NEG_INF = float("-inf")


def _cparams(*sem):
    return pltpu.CompilerParams(dimension_semantics=sem, vmem_limit_bytes=VMEM_LIMIT)


def _rms(x, g):
    return x * lax.rsqrt(jnp.mean(x * x, axis=-1, keepdims=True) + EPS) * g


_GELU_K0 = -2.0 * math.sqrt(2.0 / math.pi) * math.log2(math.e)
_GELU_K1 = _GELU_K0 * 0.044715


def _gelu_tanh(x):
    return x / (1.0 + jnp.exp2(x * (_GELU_K0 + _GELU_K1 * (x * x))))


def _in_proj_kernel(x_ref, g_ref, w_ref, cos_ref, sin_ref, q_ref, k_ref, v_ref, xr_ref, gr_ref):
    aw, kw, lw = q_ref.shape[1], LANE, xr_ref.shape[1]
    xn = _rms(x_ref[...], g_ref[...]).astype(BF16)
    z = jnp.dot(xn, w_ref[...], preferred_element_type=F32)
    c = cos_ref[...]
    s = sin_ref[...]
    reps = aw // LANE
    cq = jnp.concatenate([c] * reps, axis=1)
    sq = jnp.concatenate([s] * reps, axis=1)
    o = 0
    q = (z[:, o:o + aw] * cq + z[:, o + aw:o + 2 * aw] * sq) * (HEAD_DIM ** -0.5)
    o += 2 * aw
    k = z[:, o:o + kw] * c + z[:, o + kw:o + 2 * kw] * s
    o += 2 * kw
    v = z[:, o:o + kw]
    o += kw
    q_ref[...] = q.astype(BF16)
    k_ref[...] = k.astype(BF16)
    v_ref[...] = v.astype(BF16)
    xr_ref[...] = z[:, o:o + lw]
    gr_ref[...] = z[:, o + lw:o + 2 * lw]


def _in_proj(x, g, w_ext, cos, sin, seq, aw, lw):
    n, d = x.shape
    tm = ROW_TILE
    per_seq = seq // tm
    row = lambda i: (i, 0)
    const = lambda i: (0, 0)
    pos = lambda i: (i % per_seq, 0)
    return pl.pallas_call(
        _in_proj_kernel,
        grid=(n // tm,),
        in_specs=[pl.BlockSpec((tm, d), row), pl.BlockSpec((1, d), const),
                  pl.BlockSpec(w_ext.shape, const),
                  pl.BlockSpec((tm, LANE), pos), pl.BlockSpec((tm, LANE), pos)],
        out_specs=[pl.BlockSpec((tm, aw), row), pl.BlockSpec((tm, LANE), row),
                   pl.BlockSpec((tm, LANE), row), pl.BlockSpec((tm, lw), row),
                   pl.BlockSpec((tm, lw), row)],
        out_shape=[jax.ShapeDtypeStruct((n, aw), BF16), jax.ShapeDtypeStruct((n, LANE), BF16),
                   jax.ShapeDtypeStruct((n, LANE), BF16), jax.ShapeDtypeStruct((n, lw), F32),
                   jax.ShapeDtypeStruct((n, lw), F32)],
        compiler_params=_cparams("parallel"),
        name="in_proj",
    )(x, g, w_ext, cos, sin)


def _attn_kernel(sink_ref, q_ref, kp_ref, ks_ref, kn_ref, vp_ref, vs_ref, vn_ref, g_ref, o_ref, *, seq):
    qb = q_ref.shape[0]
    blk = WINDOW
    p0 = (pl.program_id(0) % (seq // qb)) * qb
    kf = jnp.concatenate([kp_ref[...], ks_ref[...], kn_ref[...]], axis=0)
    vf = jnp.concatenate([vp_ref[...], vs_ref[...], vn_ref[...]], axis=0)
    lane = lax.broadcasted_iota(jnp.int32, (blk, LANE), 1)
    row = lax.broadcasted_iota(jnp.int32, (blk, 3 * blk), 0)
    col = lax.broadcasted_iota(jnp.int32, (blk, 3 * blk), 1)
    rel = col - row
    band = (rel >= 0) & (rel <= 2 * WINDOW)
    grp = N_Q_HEADS // N_KV_HEADS
    for r in range(qb // blk):
        kpos = p0 + (r - 1) * blk + col
        valid = band & (kpos >= 0) & (kpos < seq)
        q4 = [q_ref[r * blk:(r + 1) * blk, m * LANE:(m + 1) * LANE] for m in range(grp)]
        zero = jnp.zeros_like(q4[0])
        stacked = ([jnp.where(lane < HEAD_DIM, qs, zero) for qs in q4]
                   + [jnp.where(lane >= HEAD_DIM, qs, zero) for qs in q4])
        qs = jnp.concatenate(stacked, axis=0)
        kk = kf[r * blk:(r + 3) * blk]
        vv = vf[r * blk:(r + 3) * blk]
        s_all = lax.dot_general(qs, kk, (((1,), (1,)), ((), ())), preferred_element_type=F32)
        probs = []
        for h in range(N_Q_HEADS):
            s = jnp.where(valid, s_all[h * blk:(h + 1) * blk], -1e30)
            sk = sink_ref[h]
            mx = jnp.maximum(jnp.max(s, axis=1, keepdims=True), sk)
            e = jnp.exp(s - mx)
            den = jnp.sum(e, axis=1, keepdims=True) + jnp.exp(sk - mx)
            probs.append((e * (1.0 / den)).astype(BF16))
        o_all = jnp.dot(jnp.concatenate(probs, axis=0), vv, preferred_element_type=F32)
        attn = jnp.concatenate(
            [jnp.where(lane < HEAD_DIM, o_all[m * blk:(m + 1) * blk], o_all[(grp + m) * blk:(grp + m + 1) * blk])
             for m in range(grp)], axis=1)
        o_ref[r * blk:(r + 1) * blk, :] = _rms(attn, g_ref[...]).astype(BF16)


def _attention(q, k2, v2, sink, g, seq):
    n, aw = q.shape
    qb = ROW_TILE
    per = qb // WINDOW
    nblk = n // WINDOW
    row = lambda i: (i, 0)
    prev = lambda i: (jnp.maximum(i * per - 1, 0), 0)
    nxt = lambda i: (jnp.minimum((i + 1) * per, nblk - 1), 0)
    kw = k2.shape[1]
    return pl.pallas_call(
        functools.partial(_attn_kernel, seq=seq),
        grid=(n // qb,),
        in_specs=[pl.BlockSpec(memory_space=pltpu.SMEM),
                  pl.BlockSpec((qb, aw), row),
                  pl.BlockSpec((WINDOW, kw), prev), pl.BlockSpec((qb, kw), row), pl.BlockSpec((WINDOW, kw), nxt),
                  pl.BlockSpec((WINDOW, kw), prev), pl.BlockSpec((qb, kw), row), pl.BlockSpec((WINDOW, kw), nxt),
                  pl.BlockSpec((1, aw), lambda i: (0, 0))],
        out_specs=pl.BlockSpec((qb, aw), row),
        out_shape=jax.ShapeDtypeStruct((n, aw), BF16),
        compiler_params=_cparams("parallel"),
        name="attention",
    )(sink, q, k2, k2, k2, v2, v2, v2, g)


def _lru_gates(x_ref, xp_ref, xn_ref, cw_ref, cb_ref, wg_ref, bg_ref, lam_ref, ext_ref, a_ref, u_ref,
               has_prev, has_next):
    tb, lw = x_ref.shape
    x = x_ref[...]
    ext_ref[0:SUBLANE, :] = jnp.where(has_prev, xp_ref[...], 0.0)
    ext_ref[SUBLANE:SUBLANE + tb, :] = x
    ext_ref[SUBLANE + tb:2 * SUBLANE + tb, :] = jnp.where(has_next, xn_ref[...], 0.0)
    cw = cw_ref[...]
    xc = cb_ref[...] + cw[2:3] * x
    for j in (0, 1, 3):
        off = SUBLANE + j - CONV_LEFT
        xc = xc + cw[j:j + 1] * ext_ref[off:off + tb, :]
    gates = jnp.dot(xc.astype(BF16), wg_ref[...], preferred_element_type=F32) + bg_ref[...]
    gate_r = jax.nn.sigmoid(gates[:, :lw])
    gate_i = jax.nn.sigmoid(gates[:, lw:])
    lam = lam_ref[...]
    log_sig = jnp.minimum(lam, 0.0) - jnp.log1p(jnp.exp(-jnp.abs(lam)))
    log_a = LRU_C * gate_r * log_sig
    a = jnp.exp(log_a)
    a_ref[...] = a
    u_ref[...] = jnp.sqrt(-jnp.tanh(log_a) * (a * a + 1.0)) * (gate_i * xc)


def _lru_fwd_kernel(x_ref, xp_ref, xn_ref, cw_ref, cb_ref, wg_ref, bg_ref, lam_ref, h_ref,
                    ext_ref, a_ref, u_ref, carry_ref):
    j = pl.program_id(1)
    nt = pl.num_programs(1)
    tb = x_ref.shape[0]

    @pl.when(j == 0)
    def _():
        carry_ref[...] = jnp.zeros_like(carry_ref)

    _lru_gates(x_ref, xp_ref, xn_ref, cw_ref, cb_ref, wg_ref, bg_ref, lam_ref, ext_ref, a_ref, u_ref,
               j > 0, j < nt - 1)

    def body(t, h):
        h = a_ref[pl.ds(t, 1), :] * h + u_ref[pl.ds(t, 1), :]
        h_ref[pl.ds(t, 1), :] = h
        return h

    carry_ref[...] = lax.fori_loop(0, tb, body, carry_ref[...], unroll=8)


def _lru_bwd_kernel(x_ref, xp_ref, xn_ref, cw_ref, cb_ref, wg_ref, bg_ref, lam_ref, gr_ref, hf_ref, g_ref,
                    o_ref, ext_ref, a_ref, u_ref, hb_ref, carry_ref):
    j = pl.program_id(1)
    nt = pl.num_programs(1)
    tb = x_ref.shape[0]

    @pl.when(j == 0)
    def _():
        carry_ref[...] = jnp.zeros_like(carry_ref)

    _lru_gates(x_ref, xp_ref, xn_ref, cw_ref, cb_ref, wg_ref, bg_ref, lam_ref, ext_ref, a_ref, u_ref,
               j < nt - 1, j > 0)

    def body(i, h):
        t = tb - 1 - i
        h = a_ref[pl.ds(t, 1), :] * h + u_ref[pl.ds(t, 1), :]
        hb_ref[pl.ds(t, 1), :] = h
        return h

    carry_ref[...] = lax.fori_loop(0, tb, body, carry_ref[...], unroll=8)
    lru = jax.nn.gelu(gr_ref[...]) * (hf_ref[...] + hb_ref[...])
    o_ref[...] = _rms(lru, g_ref[...]).astype(BF16)


def _lru_specs(batch, seq, lw, tb, reverse):
    nt = seq // tb
    per8 = tb // SUBLANE
    n8 = batch * seq // SUBLANE

    def blk(b, j):
        return b * nt + (nt - 1 - j if reverse else j)

    main = lambda b, j: (blk(b, j), 0)
    prev = lambda b, j: (jnp.maximum(blk(b, j) * per8 - 1, 0), 0)
    nxt = lambda b, j: (jnp.minimum((blk(b, j) + 1) * per8, n8 - 1), 0)
    const = lambda b, j: (0, 0)
    specs = [pl.BlockSpec((tb, lw), main), pl.BlockSpec((SUBLANE, lw), prev), pl.BlockSpec((SUBLANE, lw), nxt),
             pl.BlockSpec((4, lw), const), pl.BlockSpec((1, lw), const),
             pl.BlockSpec((lw, 2 * lw), const), pl.BlockSpec((1, 2 * lw), const), pl.BlockSpec((1, lw), const)]
    return specs, main, const, nt


def _lru_fwd(xr, cw, cb, wg, bg, lam, batch, seq):
    n, lw = xr.shape
    tb = ROW_TILE
    specs, main, _, nt = _lru_specs(batch, seq, lw, tb, reverse=False)
    return pl.pallas_call(
        _lru_fwd_kernel,
        grid=(batch, nt),
        in_specs=specs,
        out_specs=pl.BlockSpec((tb, lw), main),
        out_shape=jax.ShapeDtypeStruct((n, lw), F32),
        scratch_shapes=[pltpu.VMEM((tb + 2 * SUBLANE, lw), F32), pltpu.VMEM((tb, lw), F32),
                        pltpu.VMEM((tb, lw), F32), pltpu.VMEM((1, lw), F32)],
        compiler_params=_cparams("parallel", "arbitrary"),
        name="lru_fwd",
    )(xr, xr, xr, cw, cb, wg, bg, lam)


def _lru_bwd(xr, cw, cb, wg, bg, lam, gr, hf, g, batch, seq):
    n, lw = xr.shape
    tb = ROW_TILE
    specs, main, const, nt = _lru_specs(batch, seq, lw, tb, reverse=True)
    specs = specs + [pl.BlockSpec((tb, lw), main), pl.BlockSpec((tb, lw), main), pl.BlockSpec((1, lw), const)]
    return pl.pallas_call(
        _lru_bwd_kernel,
        grid=(batch, nt),
        in_specs=specs,
        out_specs=pl.BlockSpec((tb, lw), main),
        out_shape=jax.ShapeDtypeStruct((n, lw), BF16),
        scratch_shapes=[pltpu.VMEM((tb + 2 * SUBLANE, lw), F32), pltpu.VMEM((tb, lw), F32),
                        pltpu.VMEM((tb, lw), F32), pltpu.VMEM((tb, lw), F32), pltpu.VMEM((1, lw), F32)],
        compiler_params=_cparams("parallel", "arbitrary"),
        name="lru_bwd",
    )(xr, xr, xr, cw, cb, wg, bg, lam, gr, hf, g)


def _out_proj_kernel(x_ref, at_ref, lr_ref, wo_ref, g_ref, wqt_ref, h1_ref, xnt_ref, qt_ref):
    aw = at_ref.shape[1]
    h1 = (x_ref[...] + jnp.dot(at_ref[...], wo_ref[0:aw, :], preferred_element_type=F32)
          + jnp.dot(lr_ref[...], wo_ref[aw:, :], preferred_element_type=F32))
    h1_ref[...] = h1
    xnt = _rms(h1, g_ref[...]).T.astype(BF16)
    xnt_ref[0] = xnt
    qt_ref[0] = jnp.dot(wqt_ref[...], xnt, preferred_element_type=F32).astype(BF16)


def _out_proj(x, attn_n, lru_n, w_out, g, wq_t):
    n, d = x.shape
    tm = PEER_TOKENS
    aw, lw = attn_n.shape[1], lru_n.shape[1]
    qd = wq_t.shape[0]
    row = lambda i: (i, 0)
    const = lambda i: (0, 0)
    tile = lambda i: (i, 0, 0)
    return pl.pallas_call(
        _out_proj_kernel,
        grid=(n // tm,),
        in_specs=[pl.BlockSpec((tm, d), row), pl.BlockSpec((tm, aw), row), pl.BlockSpec((tm, lw), row),
                  pl.BlockSpec(w_out.shape, const), pl.BlockSpec((1, d), const), pl.BlockSpec(wq_t.shape, const)],
        out_specs=[pl.BlockSpec((tm, d), row), pl.BlockSpec((1, d, tm), tile), pl.BlockSpec((1, qd, tm), tile)],
        out_shape=[jax.ShapeDtypeStruct((n, d), F32), jax.ShapeDtypeStruct((n // tm, d, tm), BF16),
                   jax.ShapeDtypeStruct((n // tm, qd, tm), BF16)],
        compiler_params=_cparams("parallel"),
        name="out_proj",
    )(x, attn_n, lru_n, w_out, g, wq_t)


def _batcher_network(lo, hi):
    def merge(lo, hi, r):
        step = r * 2
        if step < hi - lo:
            yield from merge(lo, hi, step)
            yield from merge(lo + r, hi, step)
            yield from [(i, i + r) for i in range(lo + r, hi - r, step)]
        else:
            yield (lo, lo + r)

    if hi - lo >= 1:
        mid = lo + (hi - lo) // 2
        yield from _batcher_network(lo, mid)
        yield from _batcher_network(mid + 1, hi)
        yield from merge(lo, hi, 1)


def _exchange(x, i, j):
    x[i], x[j] = jnp.maximum(x[i], x[j]), jnp.minimum(x[i], x[j])


def _sorted_top16(groups):
    n = PEER_TOPK
    assert len(groups) == n
    x = list(groups)
    for i, j in _batcher_network(0, n - 1):
        _exchange(x, i, j)
    shift = SUBLANE // 2
    while shift:
        rolled = [pltpu.roll(v, shift, 0) for v in x]
        x = [jnp.maximum(x[k], rolled[n - 1 - k]) for k in range(n)]
        d = n // 2
        while d:
            for k in range(n):
                if not k & d:
                    _exchange(x, k, k + d)
            d //= 2
        shift //= 2
    return x


def _prefix_count(test, rows):
    w = jnp.where
    g1 = test(rows[7])
    g2 = test(w(g1, rows[11], rows[3]))
    g3 = test(w(g1, w(g2, rows[13], rows[9]), w(g2, rows[5], rows[1])))
    g4 = test(w(g1, w(g2, w(g3, rows[14], rows[12]), w(g3, rows[10], rows[8])),
                w(g2, w(g3, rows[6], rows[4]), w(g3, rows[2], rows[0]))))
    cnt = w(g1, 8.0, 0.0) + w(g2, 4.0, 0.0) + w(g3, 2.0, 0.0) + w(g4, 1.0, 0.0)
    return w(test(rows[15]), 16.0, cnt)


def _peer_select(qt_ref, keys_ref, s_ref, a_ref, e1_ref, ell_ref, rank_ref, e2_ref):
    nhp = keys_ref.shape[0]
    nh = nhp // 2
    t = qt_ref.shape[2]
    nch = t // LANE
    dk = keys_ref.shape[2]

    for hp in range(nhp):
        s_ref[hp] = jnp.dot(keys_ref[hp], qt_ref[0, hp * dk:(hp + 1) * dk, :], preferred_element_type=F32)

    nk = keys_ref.shape[1]
    ngrp = nk // SUBLANE

    def topk(h, _):
        for half in range(2):
            hp = 2 * h + half
            for c in range(nch):
                cs = slice(c * LANE, (c + 1) * LANE)
                s0 = s_ref[hp, :, cs]
                groups = [s0[g * SUBLANE:(g + 1) * SUBLANE] for g in range(ngrp)]
                top = _sorted_top16(groups)
                for k in range(PEER_TOPK):
                    a_ref[hp, k:k + 1, cs] = top[k][0:1]
                e = jnp.exp(s0 - top[0][0:1])
                if half == 0:
                    e1_ref[h, :, cs] = e
                else:
                    e2_ref[h, :, cs] = e.astype(GATE_DTYPE)
                    rank = [_prefix_count(lambda thr, v=v: thr > v, top) for v in groups]
                    rank_ref[h, :, cs] = jnp.concatenate(rank, axis=0).astype(GATE_DTYPE)
        return 0

    lax.fori_loop(0, nh, topk, 0)

    rows8 = lax.broadcasted_iota(jnp.int32, (SUBLANE, LANE), 0)

    def cands(h, _):
        for c in range(nch):
            cs = slice(c * LANE, (c + 1) * LANE)
            a = a_ref[2 * h, :, cs]
            b = a_ref[2 * h + 1, :, cs]
            a_lo, b_lo = a[0:SUBLANE], b[0:SUBLANE]
            blocks = [a[0:1] + b_lo, a[0:1] + b[SUBLANE:], a[SUBLANE:] + b[0:1]]
            blocks += [a[i:i + 1] + b_lo for i in (1, 2, 3)]
            blocks += [jnp.where(rows8 >= 4, a_lo + b[j:j + 1], NEG_INF) for j in (0, 1, 2)]
            top = a[0:1] + b[0:1]
            z = jnp.zeros((1, LANE), F32)
            m = top
            for k in range(PEER_TOPK):
                m = blocks[0]
                for blk in blocks[1:]:
                    m = jnp.maximum(m, blk)
                m = jnp.max(m, axis=0, keepdims=True)
                z = z + jnp.exp(m - top)
                if k + 1 < PEER_TOPK:
                    blocks = [jnp.where(blk == m, NEG_INF, blk) for blk in blocks]
            tau = jnp.broadcast_to(m, (SUBLANE, LANE))
            e1_ref[h, :, cs] = e1_ref[h, :, cs] * (1.0 / z)
            b_rows = [jnp.broadcast_to(b[r:r + 1], (SUBLANE, LANE)) for r in range(PEER_TOPK)]
            s1 = s_ref[2 * h, :, cs]
            ell = [_prefix_count(lambda thr, v=s1[g * SUBLANE:(g + 1) * SUBLANE]: v + thr >= tau, b_rows)
                   for g in range(ngrp)]
            ell_ref[h, :, cs] = jnp.concatenate(ell, axis=0)
        return 0

    lax.fori_loop(0, nh, cands, 0)


def _peer_kernel(h1_ref, xnt_ref, qt_ref, keys_ref, u_ref, vt_ref, o_ref,
                 s_ref, a_ref, e1_ref, ell_ref, rank_ref, e2_ref, p_ref, acc_ref):
    eb = pl.program_id(1)
    nsub = u_ref.shape[0]
    sub = u_ref.shape[1]
    nk = keys_ref.shape[1]
    nh = keys_ref.shape[0] // 2
    t = xnt_ref.shape[2]
    nch = t // LANE
    per_sub = sub // nk
    assert SUBLANE % per_sub == 0 and nsub % (SUBLANE // per_sub) == 0
    ngrp = nk // GATE_ROWS

    @pl.when(eb == 0)
    def _():
        _peer_select(qt_ref, keys_ref, s_ref, a_ref, e1_ref, ell_ref, rank_ref, e2_ref)
        acc_ref[...] = jnp.zeros_like(acc_ref)

    xnt = xnt_ref[0]
    acts = {}

    def expert_acts(sb):
        acts[sb] = jnp.dot(u_ref[sb], xnt, preferred_element_type=F32)

    def gate(sb):
        act = acts.pop(sb)
        i0 = pl.multiple_of((eb * nsub + sb) * per_sub // SUBLANE * SUBLANE, SUBLANE)
        r0 = sb * per_sub % SUBLANE
        for c in range(nch):
            cs = slice(c * LANE, (c + 1) * LANE)
            ell8 = [ell_ref[h, pl.ds(i0, SUBLANE), cs] for h in range(nh)]
            e18 = [e1_ref[h, pl.ds(i0, SUBLANE), cs] for h in range(nh)]
            for ii0 in range(0, per_sub, GATE_KEYS):
                keys = range(ii0, ii0 + GATE_KEYS)
                w = {ii: [None] * ngrp for ii in keys}
                for h in range(nh):
                    rows = {ii: (jnp.broadcast_to(ell8[h][r0 + ii:r0 + ii + 1], (GATE_ROWS, LANE)).astype(GATE_DTYPE),
                                 jnp.broadcast_to(e18[h][r0 + ii:r0 + ii + 1], (GATE_ROWS, LANE)).astype(GATE_DTYPE))
                            for ii in keys}
                    for g in range(ngrp):
                        gs = slice(g * GATE_ROWS, (g + 1) * GATE_ROWS)
                        rank = rank_ref[h, gs, cs]
                        e2 = e2_ref[h, gs, cs]
                        for ii in keys:
                            ell, e1 = rows[ii]
                            term = jnp.where(rank < ell, e2 * e1, jnp.zeros_like(e1))
                            w[ii][g] = term if h == 0 else w[ii][g] + term
                for ii in keys:
                    ga = _gelu_tanh(act[ii * nk:(ii + 1) * nk, cs]).astype(GATE_DTYPE)
                    gated = [w[ii][g] * ga[g * GATE_ROWS:(g + 1) * GATE_ROWS] for g in range(ngrp)]
                    p_ref[sb * sub + ii * nk:sb * sub + (ii + 1) * nk, cs] = (
                        jnp.concatenate(gated, axis=0).astype(BF16))

    expert_acts(0)
    if nsub > 1:
        expert_acts(1)
    for sb in range(nsub):
        gate(sb)
        if sb + 2 < nsub:
            expert_acts(sb + 2)
    acc_ref[...] += jnp.dot(vt_ref[0], p_ref[...], preferred_element_type=F32)

    @pl.when(eb == pl.num_programs(1) - 1)
    def _():
        o_ref[...] = h1_ref[...] + acc_ref[...].T


def _peer(h1, xn_t, q_t, keys, u3, vt3):
    n, d = h1.shape
    nt, _, t = xn_t.shape
    nblk, sub, _ = u3.shape
    nsub = PEER_SUBS_PER_STEP
    nhp, nk, dk = keys.shape
    tile = lambda i, e: (i, 0, 0)
    return pl.pallas_call(
        _peer_kernel,
        grid=(nt, nblk // nsub),
        in_specs=[pl.BlockSpec((t, d), lambda i, e: (i, 0)),
                  pl.BlockSpec((1, d, t), tile), pl.BlockSpec((1, q_t.shape[1], t), tile),
                  pl.BlockSpec(keys.shape, lambda i, e: (0, 0, 0)),
                  pl.BlockSpec((nsub, sub, d), lambda i, e: (e, 0, 0)),
                  pl.BlockSpec((1, d, nsub * sub), lambda i, e: (e, 0, 0))],
        out_specs=pl.BlockSpec((t, d), lambda i, e: (i, 0)),
        out_shape=jax.ShapeDtypeStruct((n, d), F32),
        scratch_shapes=[pltpu.VMEM((nhp, nk, t), F32), pltpu.VMEM((nhp, PEER_TOPK, t), F32),
                        pltpu.VMEM((nhp // 2, nk, t), F32), pltpu.VMEM((nhp // 2, nk, t), F32),
                        pltpu.VMEM((nhp // 2, nk, t), GATE_DTYPE), pltpu.VMEM((nhp // 2, nk, t), GATE_DTYPE),
                        pltpu.VMEM((nsub * sub, t), BF16), pltpu.VMEM((d, t), F32)],
        compiler_params=_cparams("parallel", "arbitrary"),
        name="peer",
    )(h1, xn_t, q_t, keys, u3, vt3)


def _final_kernel(h_ref, p_ref, gp_ref, wg_ref, wp_ref, gf_ref, y_ref):
    h = h_ref[...]
    gate = jax.nn.sigmoid(jnp.dot(_rms(h, gp_ref[...]).astype(BF16), wg_ref[...], preferred_element_type=F32))
    h = h + gate * jnp.dot(p_ref[...].astype(BF16), wp_ref[...], preferred_element_type=F32)
    y_ref[...] = _rms(h, gf_ref[...])


def _final(h2, p, g_ple, w_gate, w_proj, g_final):
    n, d = h2.shape
    tm = ROW_TILE
    pd = p.shape[1]
    row = lambda i: (i, 0)
    const = lambda i: (0, 0)
    return pl.pallas_call(
        _final_kernel,
        grid=(n // tm,),
        in_specs=[pl.BlockSpec((tm, d), row), pl.BlockSpec((tm, pd), row), pl.BlockSpec((1, d), const),
                  pl.BlockSpec(w_gate.shape, const), pl.BlockSpec(w_proj.shape, const), pl.BlockSpec((1, d), const)],
        out_specs=pl.BlockSpec((tm, d), row),
        out_shape=jax.ShapeDtypeStruct((n, d), F32),
        compiler_params=_cparams("parallel"),
        name="final",
    )(h2, p, g_ple, w_gate, w_proj, g_final)


def _rot_half_cols(w):
    d, width = w.shape
    w4 = w.reshape(d, width // HEAD_DIM, 2, HEAD_DIM // 2)
    return jnp.concatenate([-w4[:, :, 1:2], w4[:, :, 0:1]], axis=2).reshape(d, width)


def _block_diag(w):
    nb, bi, bj = w.shape
    eye = jnp.eye(nb, dtype=w.dtype)
    return (eye[:, None, :, None] * w[:, :, None, :]).reshape(nb * bi, nb * bj)


def _rope_tables(seq):
    half = HEAD_DIM // 2
    inv_freq = ROPE_THETA ** (-jnp.arange(half, dtype=F32) / half)
    ang = jnp.arange(seq, dtype=F32)[:, None] * inv_freq[None, :]
    reps = LANE // half
    return jnp.tile(jnp.cos(ang), (1, reps)), jnp.tile(jnp.sin(ang), (1, reps))


def _encoder(x, p, wts):
    b, s, d = x.shape
    n = b * s
    aw, lw = wts["aw"], wts["lw"]
    cos, sin = _rope_tables(s)
    x2 = x.reshape(n, d)
    q, k2, v2, xr, gr = _in_proj(x2, wts["mix_g"], wts["w_in"], cos, sin, s, aw, lw)
    attn_n = _attention(q, k2, v2, wts["sink"], wts["attn_g"], s)
    hf = _lru_fwd(xr, wts["conv_w"], wts["conv_b"], wts["wg"][0], wts["bg"][0], wts["lam"][0], b, s)
    lru_n = _lru_bwd(xr, wts["conv_w"], wts["conv_b"], wts["wg"][1], wts["bg"][1], wts["lam"][1],
                     gr, hf, wts["lru_g"], b, s)
    h1, xn_t, q_t = _out_proj(x2, attn_n, lru_n, wts["w_out"], wts["ffn_g"], wts["wq_t"])
    h2 = _peer(h1, xn_t, q_t, wts["keys"], wts["u3"], wts["vt3"])
    y = _final(h2, p.reshape(n, -1), wts["ple_g"], wts["w_gate"], wts["w_proj"], wts["final_g"])
    return y.reshape(b, s, d)


def kernel(x_prompt, x_sample, p_prompt, p_sample, mix_norm_g, w_in, attn_sink, conv_w, conv_b, lru_wa, lru_ba, lru_wx, lru_bx, lru_lambda, attn_out_norm_g, lru_out_norm_g, w_out, ffn_norm_g, peer_wq, peer_keys, peer_u, peer_v, ple_norm_g, ple_w_gate, ple_w_proj, final_norm_g):
    depth = w_in.shape[0]
    assert depth == 1, "single-layer encoder"
    l = 0
    d = w_in.shape[1]
    aw = N_Q_HEADS * HEAD_DIM
    kw = N_KV_HEADS * HEAD_DIM
    lw = conv_w.shape[2]
    assert kw == LANE and aw % LANE == 0
    o1, o2, o3, o4 = aw, aw + kw, aw + 2 * kw, aw + 2 * kw + lw
    w = w_in[l]
    wq_, wk_, wv_, wx_, wg_ = w[:, :o1], w[:, o1:o2], w[:, o2:o3], w[:, o3:o4], w[:, o4:]
    grp = N_Q_HEADS // N_KV_HEADS
    head_order = np.array([m + kv * grp for m in range(grp) for kv in range(N_KV_HEADS)])
    feat_order = (head_order[:, None] * HEAD_DIM + np.arange(HEAD_DIM)[None, :]).reshape(-1)
    wq_ = wq_[:, feat_order]
    w_out_l = jnp.concatenate([w_out[l][:aw][feat_order], w_out[l][aw:]], axis=0)
    w_ext = jnp.concatenate([wq_, _rot_half_cols(wq_), wk_, _rot_half_cols(wk_), wv_, wx_, wg_], axis=1).astype(BF16)
    nh, _, nk, dk = peer_keys.shape[1:]
    n_exp = peer_u.shape[1]
    assert n_exp == nk * nk and nk == LANE and dk == LANE
    nblk = n_exp // PEER_SUB
    wts = dict(
        aw=aw, lw=lw,
        mix_g=mix_norm_g[l][None], w_in=w_ext, sink=attn_sink[l],
        attn_g=attn_out_norm_g[l][feat_order][None], lru_g=lru_out_norm_g[l][None],
        conv_w=conv_w[l], conv_b=conv_b[l][None],
        wg=[jnp.concatenate([_block_diag(lru_wa[l, k]), _block_diag(lru_wx[l, k])], axis=1).astype(BF16) for k in range(2)],
        bg=[jnp.concatenate([lru_ba[l, k], lru_bx[l, k]])[None] for k in range(2)],
        lam=[lru_lambda[l, k][None] for k in range(2)],
        w_out=w_out_l.astype(BF16), ffn_g=ffn_norm_g[l][None],
        wq_t=peer_wq[l].T.astype(BF16),
        keys=peer_keys[l].reshape(nh * 2, nk, dk).astype(BF16),
        u3=peer_u[l].astype(BF16).reshape(nblk, PEER_SUB, d),
        vt3=peer_v[l].astype(BF16).reshape(nblk // PEER_SUBS_PER_STEP, PEER_SUBS_PER_STEP * PEER_SUB, d).transpose(0, 2, 1),
        ple_g=ple_norm_g[l][None], w_gate=ple_w_gate[l].astype(BF16), w_proj=ple_w_proj[l].astype(BF16),
        final_g=final_norm_g[None],
    )
    y_prompt = _encoder(x_prompt, p_prompt[l], wts)
    y_sample = _encoder(x_sample, p_sample[l], wts)
    return (y_prompt, y_sample)
```

```python
import functools
import math

import jax
import jax.numpy as jnp
import numpy as np
from jax import lax
from jax.experimental import pallas as pl
from jax.experimental.pallas import tpu as pltpu

F32 = jnp.float32
BF16 = jnp.bfloat16

EPS = 1e-6
N_Q_HEADS = 8
N_KV_HEADS = 2
HEAD_DIM = 64
WINDOW = 128
ROPE_THETA = 10000.0
LRU_C = 8.0
CONV_LEFT = 2
PEER_TOPK = 16

LANE = 128
SUBLANE = 8
GATE_DTYPE = jnp.bfloat16
GATE_ROWS = 16
GATE_KEYS = 2
ROW_TILE = 1024
PEER_TOKENS = 512
PEER_SUB = 512
PEER_SUBS_PER_STEP = 4
VMEM_LIMIT = 56 * 1024 * 1024
NEG_INF = float("-inf")


def _cparams(*sem):
    return pltpu.CompilerParams(dimension_semantics=sem, vmem_limit_bytes=VMEM_LIMIT)


def _rms(x, g):
    return x * lax.rsqrt(jnp.mean(x * x, axis=-1, keepdims=True) + EPS) * g


_GELU_K0 = -2.0 * math.sqrt(2.0 / math.pi) * math.log2(math.e)
_GELU_K1 = _GELU_K0 * 0.044715


def _gelu_tanh(x):
    return x / (1.0 + jnp.exp2(x * (_GELU_K0 + _GELU_K1 * (x * x))))


def _in_proj_kernel(x_ref, g_ref, w_ref, cos_ref, sin_ref, q_ref, k_ref, v_ref, xr_ref, gr_ref):
    aw, kw, lw = q_ref.shape[1], LANE, xr_ref.shape[1]
    xn = _rms(x_ref[...], g_ref[...]).astype(BF16)
    z = jnp.dot(xn, w_ref[...], preferred_element_type=F32)
    c = cos_ref[...]
    s = sin_ref[...]
    half = HEAD_DIM // 2
    first = lax.broadcasted_iota(jnp.int32, c.shape, 1) % HEAD_DIM < half

    def rope(t):
        rot = jnp.where(first, pltpu.roll(t, LANE - half, 1), pltpu.roll(t, half, 1))
        return t * c + rot * s

    for m in range(aw // LANE):
        q_ref[:, m * LANE:(m + 1) * LANE] = (rope(z[:, m * LANE:(m + 1) * LANE]) * (HEAD_DIM ** -0.5)).astype(BF16)
    o = aw
    k = rope(z[:, o:o + kw])
    o += kw
    v = z[:, o:o + kw]
    o += kw
    k_ref[...] = k.astype(BF16)
    v_ref[...] = v.astype(BF16)
    xr_ref[...] = z[:, o:o + lw]
    gr_ref[...] = z[:, o + lw:o + 2 * lw]


def _in_proj(x, g, w_ext, cos, sin, seq, aw, lw):
    n, d = x.shape
    tm = ROW_TILE
    per_seq = seq // tm
    row = lambda i: (i, 0)
    const = lambda i: (0, 0)
    pos = lambda i: (i % per_seq, 0)
    return pl.pallas_call(
        _in_proj_kernel,
        grid=(n // tm,),
        in_specs=[pl.BlockSpec((tm, d), row), pl.BlockSpec((1, d), const),
                  pl.BlockSpec(w_ext.shape, const),
                  pl.BlockSpec((tm, LANE), pos), pl.BlockSpec((tm, LANE), pos)],
        out_specs=[pl.BlockSpec((tm, aw), row), pl.BlockSpec((tm, LANE), row),
                   pl.BlockSpec((tm, LANE), row), pl.BlockSpec((tm, lw), row),
                   pl.BlockSpec((tm, lw), row)],
        out_shape=[jax.ShapeDtypeStruct((n, aw), BF16), jax.ShapeDtypeStruct((n, LANE), BF16),
                   jax.ShapeDtypeStruct((n, LANE), BF16), jax.ShapeDtypeStruct((n, lw), F32),
                   jax.ShapeDtypeStruct((n, lw), F32)],
        compiler_params=_cparams("parallel"),
        name="in_proj",
    )(x, g, w_ext, cos, sin)


def _attn_kernel(sink_ref, q_ref, kp_ref, ks_ref, kn_ref, vp_ref, vs_ref, vn_ref, g_ref, o_ref, *, seq):
    qb = q_ref.shape[0]
    blk = WINDOW
    p0 = (pl.program_id(0) % (seq // qb)) * qb
    kf = jnp.concatenate([kp_ref[...], ks_ref[...], kn_ref[...]], axis=0)
    vf = jnp.concatenate([vp_ref[...], vs_ref[...], vn_ref[...]], axis=0)
    lane = lax.broadcasted_iota(jnp.int32, (blk, LANE), 1)
    row = lax.broadcasted_iota(jnp.int32, (blk, 3 * blk), 0)
    col = lax.broadcasted_iota(jnp.int32, (blk, 3 * blk), 1)
    rel = col - row
    band = (rel >= 0) & (rel <= 2 * WINDOW)
    grp = N_Q_HEADS // N_KV_HEADS
    for r in range(qb // blk):
        kpos = p0 + (r - 1) * blk + col
        valid = band & (kpos >= 0) & (kpos < seq)
        q4 = [q_ref[r * blk:(r + 1) * blk, m * LANE:(m + 1) * LANE] for m in range(grp)]
        zero = jnp.zeros_like(q4[0])
        stacked = ([jnp.where(lane < HEAD_DIM, qs, zero) for qs in q4]
                   + [jnp.where(lane >= HEAD_DIM, qs, zero) for qs in q4])
        qs = jnp.concatenate(stacked, axis=0)
        kk = kf[r * blk:(r + 3) * blk]
        vv = vf[r * blk:(r + 3) * blk]
        s_all = lax.dot_general(qs, kk, (((1,), (1,)), ((), ())), preferred_element_type=F32)
        probs = []
        for h in range(N_Q_HEADS):
            s = jnp.where(valid, s_all[h * blk:(h + 1) * blk], -1e30)
            sk = sink_ref[h]
            mx = jnp.maximum(jnp.max(s, axis=1, keepdims=True), sk)
            e = jnp.exp(s - mx)
            den = jnp.sum(e, axis=1, keepdims=True) + jnp.exp(sk - mx)
            probs.append((e * (1.0 / den)).astype(BF16))
        o_all = jnp.dot(jnp.concatenate(probs, axis=0), vv, preferred_element_type=F32)
        attn = jnp.concatenate(
            [jnp.where(lane < HEAD_DIM, o_all[m * blk:(m + 1) * blk], o_all[(grp + m) * blk:(grp + m + 1) * blk])
             for m in range(grp)], axis=1)
        o_ref[r * blk:(r + 1) * blk, :] = _rms(attn, g_ref[...]).astype(BF16)


def _attention(q, k2, v2, sink, g, seq):
    n, aw = q.shape
    qb = ROW_TILE
    per = qb // WINDOW
    nblk = n // WINDOW
    row = lambda i: (i, 0)
    prev = lambda i: (jnp.maximum(i * per - 1, 0), 0)
    nxt = lambda i: (jnp.minimum((i + 1) * per, nblk - 1), 0)
    kw = k2.shape[1]
    return pl.pallas_call(
        functools.partial(_attn_kernel, seq=seq),
        grid=(n // qb,),
        in_specs=[pl.BlockSpec(memory_space=pltpu.SMEM),
                  pl.BlockSpec((qb, aw), row),
                  pl.BlockSpec((WINDOW, kw), prev), pl.BlockSpec((qb, kw), row), pl.BlockSpec((WINDOW, kw), nxt),
                  pl.BlockSpec((WINDOW, kw), prev), pl.BlockSpec((qb, kw), row), pl.BlockSpec((WINDOW, kw), nxt),
                  pl.BlockSpec((1, aw), lambda i: (0, 0))],
        out_specs=pl.BlockSpec((qb, aw), row),
        out_shape=jax.ShapeDtypeStruct((n, aw), BF16),
        compiler_params=_cparams("parallel"),
        name="attention",
    )(sink, q, k2, k2, k2, v2, v2, v2, g)


def _lru_gates(x_ref, xp_ref, xn_ref, cw_ref, cb_ref, wg_ref, bg_ref, lam_ref, ext_ref, a_ref, u_ref,
               has_prev, has_next):
    tb, lw = x_ref.shape
    x = x_ref[...]
    ext_ref[0:SUBLANE, :] = jnp.where(has_prev, xp_ref[...], 0.0)
    ext_ref[SUBLANE:SUBLANE + tb, :] = x
    ext_ref[SUBLANE + tb:2 * SUBLANE + tb, :] = jnp.where(has_next, xn_ref[...], 0.0)
    cw = cw_ref[...]
    xc = cb_ref[...] + cw[2:3] * x
    for j in (0, 1, 3):
        off = SUBLANE + j - CONV_LEFT
        xc = xc + cw[j:j + 1] * ext_ref[off:off + tb, :]
    gates = jnp.dot(xc.astype(BF16), wg_ref[...], preferred_element_type=F32) + bg_ref[...]
    gate_r = jax.nn.sigmoid(gates[:, :lw])
    gate_i = jax.nn.sigmoid(gates[:, lw:])
    lam = lam_ref[...]
    log_sig = jnp.minimum(lam, 0.0) - jnp.log1p(jnp.exp(-jnp.abs(lam)))
    log_a = LRU_C * gate_r * log_sig
    a = jnp.exp(log_a)
    a_ref[...] = a
    u_ref[...] = jnp.sqrt(-jnp.tanh(log_a) * (a * a + 1.0)) * (gate_i * xc)


def _lru_fwd_kernel(x_ref, xp_ref, xn_ref, cw_ref, cb_ref, wg_ref, bg_ref, lam_ref, h_ref,
                    ext_ref, a_ref, u_ref, carry_ref):
    j = pl.program_id(1)
    nt = pl.num_programs(1)
    tb = x_ref.shape[0]

    @pl.when(j == 0)
    def _():
        carry_ref[...] = jnp.zeros_like(carry_ref)

    _lru_gates(x_ref, xp_ref, xn_ref, cw_ref, cb_ref, wg_ref, bg_ref, lam_ref, ext_ref, a_ref, u_ref,
               j > 0, j < nt - 1)

    def body(t, h):
        h = a_ref[pl.ds(t, 1), :] * h + u_ref[pl.ds(t, 1), :]
        h_ref[pl.ds(t, 1), :] = h
        return h

    carry_ref[...] = lax.fori_loop(0, tb, body, carry_ref[...], unroll=8)


def _lru_bwd_kernel(x_ref, xp_ref, xn_ref, cw_ref, cb_ref, wg_ref, bg_ref, lam_ref, gr_ref, hf_ref, g_ref,
                    o_ref, ext_ref, a_ref, u_ref, hb_ref, carry_ref):
    j = pl.program_id(1)
    nt = pl.num_programs(1)
    tb = x_ref.shape[0]

    @pl.when(j == 0)
    def _():
        carry_ref[...] = jnp.zeros_like(carry_ref)

    _lru_gates(x_ref, xp_ref, xn_ref, cw_ref, cb_ref, wg_ref, bg_ref, lam_ref, ext_ref, a_ref, u_ref,
               j < nt - 1, j > 0)

    def body(i, h):
        t = tb - 1 - i
        h = a_ref[pl.ds(t, 1), :] * h + u_ref[pl.ds(t, 1), :]
        hb_ref[pl.ds(t, 1), :] = h
        return h

    carry_ref[...] = lax.fori_loop(0, tb, body, carry_ref[...], unroll=8)
    lru = _gelu_tanh(gr_ref[...]) * (hf_ref[...] + hb_ref[...])
    o_ref[...] = _rms(lru, g_ref[...]).astype(BF16)


def _lru_specs(batch, seq, lw, tb, reverse):
    nt = seq // tb
    per8 = tb // SUBLANE
    n8 = batch * seq // SUBLANE

    def blk(b, j):
        return b * nt + (nt - 1 - j if reverse else j)

    main = lambda b, j: (blk(b, j), 0)
    prev = lambda b, j: (jnp.maximum(blk(b, j) * per8 - 1, 0), 0)
    nxt = lambda b, j: (jnp.minimum((blk(b, j) + 1) * per8, n8 - 1), 0)
    const = lambda b, j: (0, 0)
    specs = [pl.BlockSpec((tb, lw), main), pl.BlockSpec((SUBLANE, lw), prev), pl.BlockSpec((SUBLANE, lw), nxt),
             pl.BlockSpec((4, lw), const), pl.BlockSpec((1, lw), const),
             pl.BlockSpec((lw, 2 * lw), const), pl.BlockSpec((1, 2 * lw), const), pl.BlockSpec((1, lw), const)]
    return specs, main, const, nt


def _lru_fwd(xr, cw, cb, wg, bg, lam, batch, seq):
    n, lw = xr.shape
    tb = ROW_TILE
    specs, main, _, nt = _lru_specs(batch, seq, lw, tb, reverse=False)
    return pl.pallas_call(
        _lru_fwd_kernel,
        grid=(batch, nt),
        in_specs=specs,
        out_specs=pl.BlockSpec((tb, lw), main),
        out_shape=jax.ShapeDtypeStruct((n, lw), F32),
        scratch_shapes=[pltpu.VMEM((tb + 2 * SUBLANE, lw), F32), pltpu.VMEM((tb, lw), F32),
                        pltpu.VMEM((tb, lw), F32), pltpu.VMEM((1, lw), F32)],
        compiler_params=_cparams("parallel", "arbitrary"),
        name="lru_fwd",
    )(xr, xr, xr, cw, cb, wg, bg, lam)


def _lru_bwd(xr, cw, cb, wg, bg, lam, gr, hf, g, batch, seq):
    n, lw = xr.shape
    tb = ROW_TILE
    specs, main, const, nt = _lru_specs(batch, seq, lw, tb, reverse=True)
    specs = specs + [pl.BlockSpec((tb, lw), main), pl.BlockSpec((tb, lw), main), pl.BlockSpec((1, lw), const)]
    return pl.pallas_call(
        _lru_bwd_kernel,
        grid=(batch, nt),
        in_specs=specs,
        out_specs=pl.BlockSpec((tb, lw), main),
        out_shape=jax.ShapeDtypeStruct((n, lw), BF16),
        scratch_shapes=[pltpu.VMEM((tb + 2 * SUBLANE, lw), F32), pltpu.VMEM((tb, lw), F32),
                        pltpu.VMEM((tb, lw), F32), pltpu.VMEM((tb, lw), F32), pltpu.VMEM((1, lw), F32)],
        compiler_params=_cparams("parallel", "arbitrary"),
        name="lru_bwd",
    )(xr, xr, xr, cw, cb, wg, bg, lam, gr, hf, g)


def _out_proj_kernel(x_ref, at_ref, lr_ref, wo_ref, g_ref, wqt_ref, h1_ref, xnt_ref, qt_ref):
    aw = at_ref.shape[1]
    h1 = (x_ref[...] + jnp.dot(at_ref[...], wo_ref[0:aw, :], preferred_element_type=F32)
          + jnp.dot(lr_ref[...], wo_ref[aw:, :], preferred_element_type=F32))
    h1_ref[...] = h1
    xnt = _rms(h1, g_ref[...]).T.astype(BF16)
    xnt_ref[0] = xnt
    qt_ref[0] = jnp.dot(wqt_ref[...], xnt, preferred_element_type=F32).astype(BF16)


def _out_proj(x, attn_n, lru_n, w_out, g, wq_t):
    n, d = x.shape
    tm = PEER_TOKENS
    aw, lw = attn_n.shape[1], lru_n.shape[1]
    qd = wq_t.shape[0]
    row = lambda i: (i, 0)
    const = lambda i: (0, 0)
    tile = lambda i: (i, 0, 0)
    return pl.pallas_call(
        _out_proj_kernel,
        grid=(n // tm,),
        in_specs=[pl.BlockSpec((tm, d), row), pl.BlockSpec((tm, aw), row), pl.BlockSpec((tm, lw), row),
                  pl.BlockSpec(w_out.shape, const), pl.BlockSpec((1, d), const), pl.BlockSpec(wq_t.shape, const)],
        out_specs=[pl.BlockSpec((tm, d), row), pl.BlockSpec((1, d, tm), tile), pl.BlockSpec((1, qd, tm), tile)],
        out_shape=[jax.ShapeDtypeStruct((n, d), F32), jax.ShapeDtypeStruct((n // tm, d, tm), BF16),
                   jax.ShapeDtypeStruct((n // tm, qd, tm), BF16)],
        compiler_params=_cparams("parallel"),
        name="out_proj",
    )(x, attn_n, lru_n, w_out, g, wq_t)


def _batcher_network(lo, hi):
    def merge(lo, hi, r):
        step = r * 2
        if step < hi - lo:
            yield from merge(lo, hi, step)
            yield from merge(lo + r, hi, step)
            yield from [(i, i + r) for i in range(lo + r, hi - r, step)]
        else:
            yield (lo, lo + r)

    if hi - lo >= 1:
        mid = lo + (hi - lo) // 2
        yield from _batcher_network(lo, mid)
        yield from _batcher_network(mid + 1, hi)
        yield from merge(lo, hi, 1)


def _exchange(x, i, j):
    x[i], x[j] = jnp.maximum(x[i], x[j]), jnp.minimum(x[i], x[j])


def _sorted_top16(groups):
    n = PEER_TOPK
    assert len(groups) == n
    x = list(groups)
    for i, j in _batcher_network(0, n - 1):
        _exchange(x, i, j)
    shift = SUBLANE // 2
    while shift:
        rolled = [pltpu.roll(v, shift, 0) for v in x]
        x = [jnp.maximum(x[k], rolled[n - 1 - k]) for k in range(n)]
        d = n // 2
        while d:
            for k in range(n):
                if not k & d:
                    _exchange(x, k, k + d)
            d //= 2
        shift //= 2
    return x


def _prefix_count(test, rows):
    w = jnp.where
    g1 = test(rows[7])
    g2 = test(w(g1, rows[11], rows[3]))
    g3 = test(w(g1, w(g2, rows[13], rows[9]), w(g2, rows[5], rows[1])))
    g4 = test(w(g1, w(g2, w(g3, rows[14], rows[12]), w(g3, rows[10], rows[8])),
                w(g2, w(g3, rows[6], rows[4]), w(g3, rows[2], rows[0]))))
    cnt = w(g1, 8.0, 0.0) + w(g2, 4.0, 0.0) + w(g3, 2.0, 0.0) + w(g4, 1.0, 0.0)
    return w(test(rows[15]), 16.0, cnt)


def _peer_select(qt_ref, keys_ref, s_ref, a_ref, e1_ref, ell_ref, rank_ref, e2_ref):
    nhp = keys_ref.shape[0]
    nh = nhp // 2
    t = qt_ref.shape[2]
    nch = t // LANE
    dk = keys_ref.shape[2]

    for hp in range(nhp):
        s_ref[hp] = jnp.dot(keys_ref[hp], qt_ref[0, hp * dk:(hp + 1) * dk, :], preferred_element_type=F32)

    nk = keys_ref.shape[1]
    ngrp = nk // SUBLANE

    def topk(h, _):
        for half in range(2):
            hp = 2 * h + half
            for c in range(nch):
                cs = slice(c * LANE, (c + 1) * LANE)
                s0 = s_ref[hp, :, cs]
                groups = [s0[g * SUBLANE:(g + 1) * SUBLANE] for g in range(ngrp)]
                top = _sorted_top16(groups)
                for k in range(PEER_TOPK):
                    a_ref[hp, k:k + 1, cs] = top[k][0:1]
                e = jnp.exp(s0 - top[0][0:1])
                if half == 0:
                    e1_ref[h, :, cs] = e
                else:
                    e2_ref[h, :, cs] = e.astype(GATE_DTYPE)
                    rank = [_prefix_count(lambda thr, v=v: thr > v, top) for v in groups]
                    rank_ref[h, :, cs] = jnp.concatenate(rank, axis=0).astype(GATE_DTYPE)
        return 0

    lax.fori_loop(0, nh, topk, 0)

    rows8 = lax.broadcasted_iota(jnp.int32, (SUBLANE, LANE), 0)

    def cands(h, _):
        for c in range(nch):
            cs = slice(c * LANE, (c + 1) * LANE)
            a = a_ref[2 * h, :, cs]
            b = a_ref[2 * h + 1, :, cs]
            a_lo, b_lo = a[0:SUBLANE], b[0:SUBLANE]
            blocks = [a[0:1] + b_lo, a[0:1] + b[SUBLANE:], a[SUBLANE:] + b[0:1]]
            blocks += [a[i:i + 1] + b_lo for i in (1, 2, 3)]
            blocks += [jnp.where(rows8 >= 4, a_lo + b[j:j + 1], NEG_INF) for j in (0, 1, 2)]
            top = a[0:1] + b[0:1]
            z = jnp.zeros((1, LANE), F32)
            m = top
            for k in range(PEER_TOPK):
                m = blocks[0]
                for blk in blocks[1:]:
                    m = jnp.maximum(m, blk)
                m = jnp.max(m, axis=0, keepdims=True)
                z = z + jnp.exp(m - top)
                if k + 1 < PEER_TOPK:
                    blocks = [jnp.where(blk == m, NEG_INF, blk) for blk in blocks]
            tau = jnp.broadcast_to(m, (SUBLANE, LANE))
            e1_ref[h, :, cs] = e1_ref[h, :, cs] * (1.0 / z)
            b_rows = [jnp.broadcast_to(b[r:r + 1], (SUBLANE, LANE)) for r in range(PEER_TOPK)]
            s1 = s_ref[2 * h, :, cs]
            ell = [_prefix_count(lambda thr, v=s1[g * SUBLANE:(g + 1) * SUBLANE]: v + thr >= tau, b_rows)
                   for g in range(ngrp)]
            ell_ref[h, :, cs] = jnp.concatenate(ell, axis=0)
        return 0

    lax.fori_loop(0, nh, cands, 0)


def _peer_kernel(h1_ref, xnt_ref, qt_ref, keys_ref, u_ref, vt_ref, o_ref,
                 s_ref, a_ref, e1_ref, ell_ref, rank_ref, e2_ref, p_ref, acc_ref):
    eb = pl.program_id(1)
    nsub = u_ref.shape[0]
    sub = u_ref.shape[1]
    nk = keys_ref.shape[1]
    nh = keys_ref.shape[0] // 2
    t = xnt_ref.shape[2]
    nch = t // LANE
    per_sub = sub // nk
    assert SUBLANE % per_sub == 0 and nsub % (SUBLANE // per_sub) == 0
    ngrp = nk // GATE_ROWS

    @pl.when(eb == 0)
    def _():
        _peer_select(qt_ref, keys_ref, s_ref, a_ref, e1_ref, ell_ref, rank_ref, e2_ref)
        acc_ref[...] = jnp.zeros_like(acc_ref)

    xnt = xnt_ref[0]
    acts = {}

    def expert_acts(sb):
        acts[sb] = jnp.dot(u_ref[sb], xnt, preferred_element_type=F32)

    def gate(sb):
        act = acts.pop(sb)
        i0 = pl.multiple_of((eb * nsub + sb) * per_sub // SUBLANE * SUBLANE, SUBLANE)
        r0 = sb * per_sub % SUBLANE
        for c in range(nch):
            cs = slice(c * LANE, (c + 1) * LANE)
            ell8 = [ell_ref[h, pl.ds(i0, SUBLANE), cs] for h in range(nh)]
            e18 = [e1_ref[h, pl.ds(i0, SUBLANE), cs] for h in range(nh)]
            for ii0 in range(0, per_sub, GATE_KEYS):
                keys = range(ii0, ii0 + GATE_KEYS)
                w = {ii: [None] * ngrp for ii in keys}
                for h in range(nh):
                    rows = {ii: (jnp.broadcast_to(ell8[h][r0 + ii:r0 + ii + 1], (GATE_ROWS, LANE)).astype(GATE_DTYPE),
                                 jnp.broadcast_to(e18[h][r0 + ii:r0 + ii + 1], (GATE_ROWS, LANE)).astype(GATE_DTYPE))
                            for ii in keys}
                    for g in range(ngrp):
                        gs = slice(g * GATE_ROWS, (g + 1) * GATE_ROWS)
                        rank = rank_ref[h, gs, cs]
                        e2 = e2_ref[h, gs, cs]
                        for ii in keys:
                            ell, e1 = rows[ii]
                            term = jnp.where(rank < ell, e2 * e1, jnp.zeros_like(e1))
                            w[ii][g] = term if h == 0 else w[ii][g] + term
                for ii in keys:
                    ga = _gelu_tanh(act[ii * nk:(ii + 1) * nk, cs]).astype(GATE_DTYPE)
                    gated = [w[ii][g] * ga[g * GATE_ROWS:(g + 1) * GATE_ROWS] for g in range(ngrp)]
                    p_ref[sb * sub + ii * nk:sb * sub + (ii + 1) * nk, cs] = (
                        jnp.concatenate(gated, axis=0).astype(BF16))

    expert_acts(0)
    if nsub > 1:
        expert_acts(1)
    for sb in range(nsub):
        gate(sb)
        if sb + 2 < nsub:
            expert_acts(sb + 2)
    acc_ref[...] += jnp.dot(vt_ref[0], p_ref[...], preferred_element_type=F32)

    @pl.when(eb == pl.num_programs(1) - 1)
    def _():
        o_ref[...] = h1_ref[...] + acc_ref[...].T


def _peer(h1, xn_t, q_t, keys, u3, vt3):
    n, d = h1.shape
    nt, _, t = xn_t.shape
    nblk, sub, _ = u3.shape
    nsub = PEER_SUBS_PER_STEP
    nhp, nk, dk = keys.shape
    tile = lambda i, e: (i, 0, 0)
    return pl.pallas_call(
        _peer_kernel,
        grid=(nt, nblk // nsub),
        in_specs=[pl.BlockSpec((t, d), lambda i, e: (i, 0)),
                  pl.BlockSpec((1, d, t), tile), pl.BlockSpec((1, q_t.shape[1], t), tile),
                  pl.BlockSpec(keys.shape, lambda i, e: (0, 0, 0)),
                  pl.BlockSpec((nsub, sub, d), lambda i, e: (e, 0, 0)),
                  pl.BlockSpec((1, d, nsub * sub), lambda i, e: (e, 0, 0))],
        out_specs=pl.BlockSpec((t, d), lambda i, e: (i, 0)),
        out_shape=jax.ShapeDtypeStruct((n, d), F32),
        scratch_shapes=[pltpu.VMEM((nhp, nk, t), F32), pltpu.VMEM((nhp, PEER_TOPK, t), F32),
                        pltpu.VMEM((nhp // 2, nk, t), F32), pltpu.VMEM((nhp // 2, nk, t), F32),
                        pltpu.VMEM((nhp // 2, nk, t), GATE_DTYPE), pltpu.VMEM((nhp // 2, nk, t), GATE_DTYPE),
                        pltpu.VMEM((nsub * sub, t), BF16), pltpu.VMEM((d, t), F32)],
        compiler_params=_cparams("parallel", "arbitrary"),
        name="peer",
    )(h1, xn_t, q_t, keys, u3, vt3)


def _final_kernel(h_ref, p_ref, gp_ref, wg_ref, wp_ref, gf_ref, y_ref):
    h = h_ref[...]
    gate = jax.nn.sigmoid(jnp.dot(_rms(h, gp_ref[...]).astype(BF16), wg_ref[...], preferred_element_type=F32))
    h = h + gate * jnp.dot(p_ref[...].astype(BF16), wp_ref[...], preferred_element_type=F32)
    y_ref[...] = _rms(h, gf_ref[...])


def _final(h2, p, g_ple, w_gate, w_proj, g_final):
    n, d = h2.shape
    tm = ROW_TILE
    pd = p.shape[1]
    row = lambda i: (i, 0)
    const = lambda i: (0, 0)
    return pl.pallas_call(
        _final_kernel,
        grid=(n // tm,),
        in_specs=[pl.BlockSpec((tm, d), row), pl.BlockSpec((tm, pd), row), pl.BlockSpec((1, d), const),
                  pl.BlockSpec(w_gate.shape, const), pl.BlockSpec(w_proj.shape, const), pl.BlockSpec((1, d), const)],
        out_specs=pl.BlockSpec((tm, d), row),
        out_shape=jax.ShapeDtypeStruct((n, d), F32),
        compiler_params=_cparams("parallel"),
        name="final",
    )(h2, p, g_ple, w_gate, w_proj, g_final)


def _block_diag(w):
    nb, bi, bj = w.shape
    eye = jnp.eye(nb, dtype=w.dtype)
    return (eye[:, None, :, None] * w[:, :, None, :]).reshape(nb * bi, nb * bj)


def _rope_tables(seq):
    half = HEAD_DIM // 2
    inv_freq = ROPE_THETA ** (-jnp.arange(half, dtype=F32) / half)
    ang = jnp.arange(seq, dtype=F32)[:, None] * inv_freq[None, :]
    sin = jnp.sin(ang)
    heads = LANE // HEAD_DIM
    return jnp.tile(jnp.cos(ang), (1, 2 * heads)), jnp.tile(jnp.concatenate([-sin, sin], axis=1), (1, heads))


def _encoder(x, p, wts):
    b, s, d = x.shape
    n = b * s
    aw, lw = wts["aw"], wts["lw"]
    cos, sin = _rope_tables(s)
    x2 = x.reshape(n, d)
    q, k2, v2, xr, gr = _in_proj(x2, wts["mix_g"], wts["w_in"], cos, sin, s, aw, lw)
    attn_n = _attention(q, k2, v2, wts["sink"], wts["attn_g"], s)
    hf = _lru_fwd(xr, wts["conv_w"], wts["conv_b"], wts["wg"][0], wts["bg"][0], wts["lam"][0], b, s)
    lru_n = _lru_bwd(xr, wts["conv_w"], wts["conv_b"], wts["wg"][1], wts["bg"][1], wts["lam"][1],
                     gr, hf, wts["lru_g"], b, s)
    h1, xn_t, q_t = _out_proj(x2, attn_n, lru_n, wts["w_out"], wts["ffn_g"], wts["wq_t"])
    h2 = _peer(h1, xn_t, q_t, wts["keys"], wts["u3"], wts["vt3"])
    y = _final(h2, p.reshape(n, -1), wts["ple_g"], wts["w_gate"], wts["w_proj"], wts["final_g"])
    return y.reshape(b, s, d)


def kernel(x_prompt, x_sample, p_prompt, p_sample, mix_norm_g, w_in, attn_sink, conv_w, conv_b, lru_wa, lru_ba, lru_wx, lru_bx, lru_lambda, attn_out_norm_g, lru_out_norm_g, w_out, ffn_norm_g, peer_wq, peer_keys, peer_u, peer_v, ple_norm_g, ple_w_gate, ple_w_proj, final_norm_g):
    depth = w_in.shape[0]
    assert depth == 1, "single-layer encoder"
    l = 0
    d = w_in.shape[1]
    aw = N_Q_HEADS * HEAD_DIM
    kw = N_KV_HEADS * HEAD_DIM
    lw = conv_w.shape[2]
    assert kw == LANE and aw % LANE == 0
    o1, o2, o3, o4 = aw, aw + kw, aw + 2 * kw, aw + 2 * kw + lw
    w = w_in[l]
    wq_, wk_, wv_, wx_, wg_ = w[:, :o1], w[:, o1:o2], w[:, o2:o3], w[:, o3:o4], w[:, o4:]
    grp = N_Q_HEADS // N_KV_HEADS
    head_order = np.array([m + kv * grp for m in range(grp) for kv in range(N_KV_HEADS)])
    feat_order = (head_order[:, None] * HEAD_DIM + np.arange(HEAD_DIM)[None, :]).reshape(-1)
    wq_ = wq_[:, feat_order]
    w_out_l = jnp.concatenate([w_out[l][:aw][feat_order], w_out[l][aw:]], axis=0)
    w_ext = jnp.concatenate([wq_, wk_, wv_, wx_, wg_], axis=1).astype(BF16)
    nh, _, nk, dk = peer_keys.shape[1:]
    n_exp = peer_u.shape[1]
    assert n_exp == nk * nk and nk == LANE and dk == LANE
    nblk = n_exp // PEER_SUB
    wts = dict(
        aw=aw, lw=lw,
        mix_g=mix_norm_g[l][None], w_in=w_ext, sink=attn_sink[l],
        attn_g=attn_out_norm_g[l][feat_order][None], lru_g=lru_out_norm_g[l][None],
        conv_w=conv_w[l], conv_b=conv_b[l][None],
        wg=[jnp.concatenate([_block_diag(lru_wa[l, k]), _block_diag(lru_wx[l, k])], axis=1).astype(BF16) for k in range(2)],
        bg=[jnp.concatenate([lru_ba[l, k], lru_bx[l, k]])[None] for k in range(2)],
        lam=[lru_lambda[l, k][None] for k in range(2)],
        w_out=w_out_l.astype(BF16), ffn_g=ffn_norm_g[l][None],
        wq_t=peer_wq[l].T.astype(BF16),
        keys=peer_keys[l].reshape(nh * 2, nk, dk).astype(BF16),
        u3=peer_u[l].astype(BF16).reshape(nblk, PEER_SUB, d),
        vt3=peer_v[l].astype(BF16).reshape(nblk // PEER_SUBS_PER_STEP, PEER_SUBS_PER_STEP * PEER_SUB, d).transpose(0, 2, 1),
        ple_g=ple_norm_g[l][None], w_gate=ple_w_gate[l].astype(BF16), w_proj=ple_w_proj[l].astype(BF16),
        final_g=final_norm_g[None],
    )
    y_prompt = _encoder(x_prompt, p_prompt[l], wts)
    y_sample = _encoder(x_sample, p_sample[l], wts)
    return (y_prompt, y_sample)
```

```python
import functools
import math

import jax
import jax.numpy as jnp
import numpy as np
from jax import lax
from jax.experimental import pallas as pl
from jax.experimental.pallas import tpu as pltpu

F32 = jnp.float32
BF16 = jnp.bfloat16

EPS = 1e-6
N_Q_HEADS = 8
N_KV_HEADS = 2
HEAD_DIM = 64
WINDOW = 128
ROPE_THETA = 10000.0
LRU_C = 8.0
CONV_LEFT = 2
PEER_TOPK = 16

LANE = 128
SUBLANE = 8
GATE_DTYPE = jnp.bfloat16
GATE_ROWS = 16
GATE_KEYS = 2
ROW_TILE = 1024
PEER_TOKENS = 512
PEER_SUB = 512
PEER_SUBS_PER_STEP = 4
VMEM_LIMIT = 56 * 1024 * 1024
NEG_INF = float("-inf")


def _cparams(*sem):
    return pltpu.CompilerParams(dimension_semantics=sem, vmem_limit_bytes=VMEM_LIMIT)


def _rms(x, g):
    return x * lax.rsqrt(jnp.mean(x * x, axis=-1, keepdims=True) + EPS) * g


_GELU_K0 = -2.0 * math.sqrt(2.0 / math.pi) * math.log2(math.e)
_GELU_K1 = _GELU_K0 * 0.044715


def _gelu_tanh(x):
    return x / (1.0 + jnp.exp2(x * (_GELU_K0 + _GELU_K1 * (x * x))))


def _in_proj_kernel(x_ref, g_ref, w_ref, cos_ref, sin_ref, q_ref, k_ref, v_ref, xr_ref, gr_ref):
    aw, kw, lw = q_ref.shape[1], LANE, xr_ref.shape[1]
    xn = _rms(x_ref[...], g_ref[...]).astype(BF16)
    z = jnp.dot(xn, w_ref[...], preferred_element_type=F32)
    c = cos_ref[...]
    s = sin_ref[...]
    half = HEAD_DIM // 2
    first = lax.broadcasted_iota(jnp.int32, c.shape, 1) % HEAD_DIM < half

    def rope(t):
        rot = jnp.where(first, pltpu.roll(t, LANE - half, 1), pltpu.roll(t, half, 1))
        return t * c + rot * s

    for m in range(aw // LANE):
        q_ref[:, m * LANE:(m + 1) * LANE] = (rope(z[:, m * LANE:(m + 1) * LANE]) * (HEAD_DIM ** -0.5)).astype(BF16)
    o = aw
    k = rope(z[:, o:o + kw])
    o += kw
    v = z[:, o:o + kw]
    o += kw
    k_ref[...] = k.astype(BF16)
    v_ref[...] = v.astype(BF16)
    xr_ref[...] = z[:, o:o + lw]
    gr_ref[...] = z[:, o + lw:o + 2 * lw]


def _in_proj(x, g, w_ext, cos, sin, seq, aw, lw):
    n, d = x.shape
    tm = ROW_TILE
    per_seq = seq // tm
    row = lambda i: (i, 0)
    const = lambda i: (0, 0)
    pos = lambda i: (i % per_seq, 0)
    return pl.pallas_call(
        _in_proj_kernel,
        grid=(n // tm,),
        in_specs=[pl.BlockSpec((tm, d), row), pl.BlockSpec((1, d), const),
                  pl.BlockSpec(w_ext.shape, const),
                  pl.BlockSpec((tm, LANE), pos), pl.BlockSpec((tm, LANE), pos)],
        out_specs=[pl.BlockSpec((tm, aw), row), pl.BlockSpec((tm, LANE), row),
                   pl.BlockSpec((tm, LANE), row), pl.BlockSpec((tm, lw), row),
                   pl.BlockSpec((tm, lw), row)],
        out_shape=[jax.ShapeDtypeStruct((n, aw), BF16), jax.ShapeDtypeStruct((n, LANE), BF16),
                   jax.ShapeDtypeStruct((n, LANE), BF16), jax.ShapeDtypeStruct((n, lw), F32),
                   jax.ShapeDtypeStruct((n, lw), F32)],
        compiler_params=_cparams("parallel"),
        name="in_proj",
    )(x, g, w_ext, cos, sin)


def _attn_kernel(sink_ref, q_ref, kp_ref, ks_ref, kn_ref, vp_ref, vs_ref, vn_ref, g_ref, o_ref, *, seq):
    qb = q_ref.shape[0]
    blk = WINDOW
    p0 = (pl.program_id(0) % (seq // qb)) * qb
    kf = jnp.concatenate([kp_ref[...], ks_ref[...], kn_ref[...]], axis=0)
    vf = jnp.concatenate([vp_ref[...], vs_ref[...], vn_ref[...]], axis=0)
    lane = lax.broadcasted_iota(jnp.int32, (blk, LANE), 1)
    row = lax.broadcasted_iota(jnp.int32, (blk, 3 * blk), 0)
    col = lax.broadcasted_iota(jnp.int32, (blk, 3 * blk), 1)
    rel = col - row
    band = (rel >= 0) & (rel <= 2 * WINDOW)
    grp = N_Q_HEADS // N_KV_HEADS
    for r in range(qb // blk):
        kpos = p0 + (r - 1) * blk + col
        valid = band & (kpos >= 0) & (kpos < seq)
        q4 = [q_ref[r * blk:(r + 1) * blk, m * LANE:(m + 1) * LANE] for m in range(grp)]
        zero = jnp.zeros_like(q4[0])
        stacked = ([jnp.where(lane < HEAD_DIM, qs, zero) for qs in q4]
                   + [jnp.where(lane >= HEAD_DIM, qs, zero) for qs in q4])
        qs = jnp.concatenate(stacked, axis=0)
        kk = kf[r * blk:(r + 3) * blk]
        vv = vf[r * blk:(r + 3) * blk]
        s_all = lax.dot_general(qs, kk, (((1,), (1,)), ((), ())), preferred_element_type=F32)
        probs = []
        for h in range(N_Q_HEADS):
            s = jnp.where(valid, s_all[h * blk:(h + 1) * blk], -1e30)
            sk = sink_ref[h]
            mx = jnp.maximum(jnp.max(s, axis=1, keepdims=True), sk)
            e = jnp.exp(s - mx)
            den = jnp.sum(e, axis=1, keepdims=True) + jnp.exp(sk - mx)
            probs.append((e * (1.0 / den)).astype(BF16))
        o_all = jnp.dot(jnp.concatenate(probs, axis=0), vv, preferred_element_type=F32)
        attn = jnp.concatenate(
            [jnp.where(lane < HEAD_DIM, o_all[m * blk:(m + 1) * blk], o_all[(grp + m) * blk:(grp + m + 1) * blk])
             for m in range(grp)], axis=1)
        o_ref[r * blk:(r + 1) * blk, :] = _rms(attn, g_ref[...]).astype(BF16)


def _attention(q, k2, v2, sink, g, seq):
    n, aw = q.shape
    qb = ROW_TILE
    per = qb // WINDOW
    nblk = n // WINDOW
    row = lambda i: (i, 0)
    prev = lambda i: (jnp.maximum(i * per - 1, 0), 0)
    nxt = lambda i: (jnp.minimum((i + 1) * per, nblk - 1), 0)
    kw = k2.shape[1]
    return pl.pallas_call(
        functools.partial(_attn_kernel, seq=seq),
        grid=(n // qb,),
        in_specs=[pl.BlockSpec(memory_space=pltpu.SMEM),
                  pl.BlockSpec((qb, aw), row),
                  pl.BlockSpec((WINDOW, kw), prev), pl.BlockSpec((qb, kw), row), pl.BlockSpec((WINDOW, kw), nxt),
                  pl.BlockSpec((WINDOW, kw), prev), pl.BlockSpec((qb, kw), row), pl.BlockSpec((WINDOW, kw), nxt),
                  pl.BlockSpec((1, aw), lambda i: (0, 0))],
        out_specs=pl.BlockSpec((qb, aw), row),
        out_shape=jax.ShapeDtypeStruct((n, aw), BF16),
        compiler_params=_cparams("parallel"),
        name="attention",
    )(sink, q, k2, k2, k2, v2, v2, v2, g)


def _lru_gates(x_ref, xp_ref, xn_ref, cw_ref, cb_ref, wg_ref, bg_ref, lam_ref, ext_ref, a_ref, u_ref,
               has_prev, has_next):
    tb, lw = x_ref.shape
    x = x_ref[...]
    ext_ref[0:SUBLANE, :] = jnp.where(has_prev, xp_ref[...], 0.0)
    ext_ref[SUBLANE:SUBLANE + tb, :] = x
    ext_ref[SUBLANE + tb:2 * SUBLANE + tb, :] = jnp.where(has_next, xn_ref[...], 0.0)
    cw = cw_ref[...]
    xc = cb_ref[...] + cw[2:3] * x
    for j in (0, 1, 3):
        off = SUBLANE + j - CONV_LEFT
        xc = xc + cw[j:j + 1] * ext_ref[off:off + tb, :]
    gates = jnp.dot(xc.astype(BF16), wg_ref[...], preferred_element_type=F32) + bg_ref[...]
    gate_r = jax.nn.sigmoid(gates[:, :lw])
    gate_i = jax.nn.sigmoid(gates[:, lw:])
    lam = lam_ref[...]
    log_sig = jnp.minimum(lam, 0.0) - jnp.log1p(jnp.exp(-jnp.abs(lam)))
    log_a = LRU_C * gate_r * log_sig
    a = jnp.exp(log_a)
    a_ref[...] = a
    u_ref[...] = jnp.sqrt(-jnp.tanh(log_a) * (a * a + 1.0)) * (gate_i * xc)


def _lru_fwd_kernel(x_ref, xp_ref, xn_ref, cw_ref, cb_ref, wg_ref, bg_ref, lam_ref, h_ref,
                    ext_ref, a_ref, u_ref, carry_ref):
    j = pl.program_id(1)
    nt = pl.num_programs(1)
    tb = x_ref.shape[0]

    @pl.when(j == 0)
    def _():
        carry_ref[...] = jnp.zeros_like(carry_ref)

    _lru_gates(x_ref, xp_ref, xn_ref, cw_ref, cb_ref, wg_ref, bg_ref, lam_ref, ext_ref, a_ref, u_ref,
               j > 0, j < nt - 1)

    def body(t, h):
        h = a_ref[pl.ds(t, 1), :] * h + u_ref[pl.ds(t, 1), :]
        h_ref[pl.ds(t, 1), :] = h
        return h

    carry_ref[...] = lax.fori_loop(0, tb, body, carry_ref[...], unroll=8)


def _lru_bwd_kernel(x_ref, xp_ref, xn_ref, cw_ref, cb_ref, wg_ref, bg_ref, lam_ref, gr_ref, hf_ref, g_ref,
                    o_ref, ext_ref, a_ref, u_ref, hb_ref, carry_ref):
    j = pl.program_id(1)
    nt = pl.num_programs(1)
    tb = x_ref.shape[0]

    @pl.when(j == 0)
    def _():
        carry_ref[...] = jnp.zeros_like(carry_ref)

    _lru_gates(x_ref, xp_ref, xn_ref, cw_ref, cb_ref, wg_ref, bg_ref, lam_ref, ext_ref, a_ref, u_ref,
               j < nt - 1, j > 0)

    def body(i, h):
        t = tb - 1 - i
        h = a_ref[pl.ds(t, 1), :] * h + u_ref[pl.ds(t, 1), :]
        hb_ref[pl.ds(t, 1), :] = h
        return h

    carry_ref[...] = lax.fori_loop(0, tb, body, carry_ref[...], unroll=8)
    lru = _gelu_tanh(gr_ref[...]) * (hf_ref[...] + hb_ref[...])
    o_ref[...] = _rms(lru, g_ref[...]).astype(BF16)


def _lru_specs(batch, seq, lw, tb, reverse):
    nt = seq // tb
    per8 = tb // SUBLANE
    n8 = batch * seq // SUBLANE

    def blk(b, j):
        return b * nt + (nt - 1 - j if reverse else j)

    main = lambda b, j: (blk(b, j), 0)
    prev = lambda b, j: (jnp.maximum(blk(b, j) * per8 - 1, 0), 0)
    nxt = lambda b, j: (jnp.minimum((blk(b, j) + 1) * per8, n8 - 1), 0)
    const = lambda b, j: (0, 0)
    specs = [pl.BlockSpec((tb, lw), main), pl.BlockSpec((SUBLANE, lw), prev), pl.BlockSpec((SUBLANE, lw), nxt),
             pl.BlockSpec((4, lw), const), pl.BlockSpec((1, lw), const),
             pl.BlockSpec((lw, 2 * lw), const), pl.BlockSpec((1, 2 * lw), const), pl.BlockSpec((1, lw), const)]
    return specs, main, const, nt


def _lru_fwd(xr, cw, cb, wg, bg, lam, batch, seq):
    n, lw = xr.shape
    tb = ROW_TILE
    specs, main, _, nt = _lru_specs(batch, seq, lw, tb, reverse=False)
    return pl.pallas_call(
        _lru_fwd_kernel,
        grid=(batch, nt),
        in_specs=specs,
        out_specs=pl.BlockSpec((tb, lw), main),
        out_shape=jax.ShapeDtypeStruct((n, lw), F32),
        scratch_shapes=[pltpu.VMEM((tb + 2 * SUBLANE, lw), F32), pltpu.VMEM((tb, lw), F32),
                        pltpu.VMEM((tb, lw), F32), pltpu.VMEM((1, lw), F32)],
        compiler_params=_cparams("parallel", "arbitrary"),
        name="lru_fwd",
    )(xr, xr, xr, cw, cb, wg, bg, lam)


def _lru_bwd(xr, cw, cb, wg, bg, lam, gr, hf, g, batch, seq):
    n, lw = xr.shape
    tb = ROW_TILE
    specs, main, const, nt = _lru_specs(batch, seq, lw, tb, reverse=True)
    specs = specs + [pl.BlockSpec((tb, lw), main), pl.BlockSpec((tb, lw), main), pl.BlockSpec((1, lw), const)]
    return pl.pallas_call(
        _lru_bwd_kernel,
        grid=(batch, nt),
        in_specs=specs,
        out_specs=pl.BlockSpec((tb, lw), main),
        out_shape=jax.ShapeDtypeStruct((n, lw), BF16),
        scratch_shapes=[pltpu.VMEM((tb + 2 * SUBLANE, lw), F32), pltpu.VMEM((tb, lw), F32),
                        pltpu.VMEM((tb, lw), F32), pltpu.VMEM((tb, lw), F32), pltpu.VMEM((1, lw), F32)],
        compiler_params=_cparams("parallel", "arbitrary"),
        name="lru_bwd",
    )(xr, xr, xr, cw, cb, wg, bg, lam, gr, hf, g)


def _out_proj_kernel(x_ref, at_ref, lr_ref, wo_ref, g_ref, wqt_ref, h1_ref, xnt_ref, qt_ref):
    aw = at_ref.shape[1]
    h1 = (x_ref[...] + jnp.dot(at_ref[...], wo_ref[0:aw, :], preferred_element_type=F32)
          + jnp.dot(lr_ref[...], wo_ref[aw:, :], preferred_element_type=F32))
    h1_ref[...] = h1
    xnt = _rms(h1, g_ref[...]).T.astype(BF16)
    xnt_ref[0] = xnt
    qt_ref[0] = jnp.dot(wqt_ref[...], xnt, preferred_element_type=F32).astype(BF16)


def _out_proj(x, attn_n, lru_n, w_out, g, wq_t):
    n, d = x.shape
    tm = PEER_TOKENS
    aw, lw = attn_n.shape[1], lru_n.shape[1]
    qd = wq_t.shape[0]
    row = lambda i: (i, 0)
    const = lambda i: (0, 0)
    tile = lambda i: (i, 0, 0)
    return pl.pallas_call(
        _out_proj_kernel,
        grid=(n // tm,),
        in_specs=[pl.BlockSpec((tm, d), row), pl.BlockSpec((tm, aw), row), pl.BlockSpec((tm, lw), row),
                  pl.BlockSpec(w_out.shape, const), pl.BlockSpec((1, d), const), pl.BlockSpec(wq_t.shape, const)],
        out_specs=[pl.BlockSpec((tm, d), row), pl.BlockSpec((1, d, tm), tile), pl.BlockSpec((1, qd, tm), tile)],
        out_shape=[jax.ShapeDtypeStruct((n, d), F32), jax.ShapeDtypeStruct((n // tm, d, tm), BF16),
                   jax.ShapeDtypeStruct((n // tm, qd, tm), BF16)],
        compiler_params=_cparams("parallel"),
        name="out_proj",
    )(x, attn_n, lru_n, w_out, g, wq_t)


def _batcher_network(lo, hi):
    def merge(lo, hi, r):
        step = r * 2
        if step < hi - lo:
            yield from merge(lo, hi, step)
            yield from merge(lo + r, hi, step)
            yield from [(i, i + r) for i in range(lo + r, hi - r, step)]
        else:
            yield (lo, lo + r)

    if hi - lo >= 1:
        mid = lo + (hi - lo) // 2
        yield from _batcher_network(lo, mid)
        yield from _batcher_network(mid + 1, hi)
        yield from merge(lo, hi, 1)


def _exchange(x, i, j):
    x[i], x[j] = jnp.maximum(x[i], x[j]), jnp.minimum(x[i], x[j])


def _sorted_top16(groups):
    n = PEER_TOPK
    assert len(groups) == n
    x = list(groups)
    for i, j in _batcher_network(0, n - 1):
        _exchange(x, i, j)
    shift = SUBLANE // 2
    while shift:
        rolled = [pltpu.roll(v, shift, 0) for v in x]
        x = [jnp.maximum(x[k], rolled[n - 1 - k]) for k in range(n)]
        d = n // 2
        while d:
            for k in range(n):
                if not k & d:
                    _exchange(x, k, k + d)
            d //= 2
        shift //= 2
    return x


def _prefix_count(test, rows):
    w = jnp.where
    g1 = test(rows[7])
    g2 = test(w(g1, rows[11], rows[3]))
    g3 = test(w(g1, w(g2, rows[13], rows[9]), w(g2, rows[5], rows[1])))
    g4 = test(w(g1, w(g2, w(g3, rows[14], rows[12]), w(g3, rows[10], rows[8])),
                w(g2, w(g3, rows[6], rows[4]), w(g3, rows[2], rows[0]))))
    cnt = w(g1, 8.0, 0.0) + w(g2, 4.0, 0.0) + w(g3, 2.0, 0.0) + w(g4, 1.0, 0.0)
    return w(test(rows[15]), 16.0, cnt)


def _peer_select(qt_ref, keys_ref, s_ref, a_ref, e1_ref, ell_ref, rank_ref, e2_ref):
    nhp = keys_ref.shape[0]
    nh = nhp // 2
    t = qt_ref.shape[2]
    nch = t // LANE
    dk = keys_ref.shape[2]

    for hp in range(nhp):
        s_ref[hp] = jnp.dot(keys_ref[hp], qt_ref[0, hp * dk:(hp + 1) * dk, :], preferred_element_type=F32)

    nk = keys_ref.shape[1]
    ngrp = nk // SUBLANE

    def topk(h, _):
        for half in range(2):
            hp = 2 * h + half
            for c in range(nch):
                cs = slice(c * LANE, (c + 1) * LANE)
                s0 = s_ref[hp, :, cs]
                groups = [s0[g * SUBLANE:(g + 1) * SUBLANE] for g in range(ngrp)]
                top = _sorted_top16(groups)
                for k in range(PEER_TOPK):
                    a_ref[hp, k:k + 1, cs] = top[k][0:1]
                e = jnp.exp(s0 - top[0][0:1])
                if half == 0:
                    e1_ref[h, :, cs] = e
                else:
                    e2_ref[h, :, cs] = e.astype(GATE_DTYPE)
                    rank = [_prefix_count(lambda thr, v=v: thr > v, top) for v in groups]
                    rank_ref[h, :, cs] = jnp.concatenate(rank, axis=0).astype(GATE_DTYPE)
        return 0

    lax.fori_loop(0, nh, topk, 0)

    rows8 = lax.broadcasted_iota(jnp.int32, (SUBLANE, LANE), 0)

    def cands(h, _):
        for c in range(nch):
            cs = slice(c * LANE, (c + 1) * LANE)
            a = a_ref[2 * h, :, cs]
            b = a_ref[2 * h + 1, :, cs]
            a_lo, b_lo = a[0:SUBLANE], b[0:SUBLANE]
            blocks = [a[0:1] + b_lo, a[0:1] + b[SUBLANE:], a[SUBLANE:] + b[0:1]]
            blocks += [a[i:i + 1] + b_lo for i in (1, 2, 3)]
            blocks += [jnp.where(rows8 >= 4, a_lo + b[j:j + 1], NEG_INF) for j in (0, 1, 2)]
            top = a[0:1] + b[0:1]
            z = jnp.zeros((1, LANE), F32)
            m = top
            for k in range(PEER_TOPK):
                m = blocks[0]
                for blk in blocks[1:]:
                    m = jnp.maximum(m, blk)
                m = jnp.max(m, axis=0, keepdims=True)
                z = z + jnp.exp(m - top)
                if k + 1 < PEER_TOPK:
                    blocks = [jnp.where(blk == m, NEG_INF, blk) for blk in blocks]
            tau = jnp.broadcast_to(m, (SUBLANE, LANE))
            e1_ref[h, :, cs] = e1_ref[h, :, cs] * (1.0 / z)
            b_rows = [jnp.broadcast_to(b[r:r + 1], (SUBLANE, LANE)) for r in range(PEER_TOPK)]
            s1 = s_ref[2 * h, :, cs]
            ell = [_prefix_count(lambda thr, v=s1[g * SUBLANE:(g + 1) * SUBLANE]: v + thr >= tau, b_rows)
                   for g in range(ngrp)]
            ell_ref[h, :, cs] = jnp.concatenate(ell, axis=0)
        return 0

    lax.fori_loop(0, nh, cands, 0)


def _peer_kernel(h1_ref, xnt_ref, qt_ref, keys_ref, u_ref, vt_ref, unext_ref, xnext_ref, o_ref,
                 s_ref, a_ref, e1_ref, ell_ref, rank_ref, e2_ref, p_ref, acc_ref, act0_ref):
    eb = pl.program_id(1)
    nsub = u_ref.shape[0]
    sub = u_ref.shape[1]
    nk = keys_ref.shape[1]
    nh = keys_ref.shape[0] // 2
    t = xnt_ref.shape[2]
    nch = t // LANE
    per_sub = sub // nk
    assert SUBLANE % per_sub == 0 and nsub % (SUBLANE // per_sub) == 0
    ngrp = nk // GATE_ROWS

    @pl.when(eb == 0)
    def _():
        _peer_select(qt_ref, keys_ref, s_ref, a_ref, e1_ref, ell_ref, rank_ref, e2_ref)
        acc_ref[...] = jnp.zeros_like(acc_ref)

    xnt = xnt_ref[0]

    @pl.when((pl.program_id(0) == 0) & (eb == 0))
    def _():
        act0_ref[...] = jnp.dot(u_ref[0], xnt, preferred_element_type=F32)

    acts = {0: act0_ref[...]}

    def expert_acts(sb):
        acts[sb] = jnp.dot(u_ref[sb], xnt, preferred_element_type=F32)

    def gate(sb):
        act = acts.pop(sb)
        i0 = pl.multiple_of((eb * nsub + sb) * per_sub // SUBLANE * SUBLANE, SUBLANE)
        r0 = sb * per_sub % SUBLANE
        for c in range(nch):
            cs = slice(c * LANE, (c + 1) * LANE)
            ell8 = [ell_ref[h, pl.ds(i0, SUBLANE), cs] for h in range(nh)]
            e18 = [e1_ref[h, pl.ds(i0, SUBLANE), cs] for h in range(nh)]
            for ii0 in range(0, per_sub, GATE_KEYS):
                keys = range(ii0, ii0 + GATE_KEYS)
                w = {ii: [None] * ngrp for ii in keys}
                for h in range(nh):
                    rows = {ii: (jnp.broadcast_to(ell8[h][r0 + ii:r0 + ii + 1], (GATE_ROWS, LANE)).astype(GATE_DTYPE),
                                 jnp.broadcast_to(e18[h][r0 + ii:r0 + ii + 1], (GATE_ROWS, LANE)).astype(GATE_DTYPE))
                            for ii in keys}
                    for g in range(ngrp):
                        gs = slice(g * GATE_ROWS, (g + 1) * GATE_ROWS)
                        rank = rank_ref[h, gs, cs]
                        e2 = e2_ref[h, gs, cs]
                        for ii in keys:
                            ell, e1 = rows[ii]
                            term = jnp.where(rank < ell, e2 * e1, jnp.zeros_like(e1))
                            w[ii][g] = term if h == 0 else w[ii][g] + term
                for ii in keys:
                    ga = _gelu_tanh(act[ii * nk:(ii + 1) * nk, cs]).astype(GATE_DTYPE)
                    gated = [w[ii][g] * ga[g * GATE_ROWS:(g + 1) * GATE_ROWS] for g in range(ngrp)]
                    p_ref[sb * sub + ii * nk:sb * sub + (ii + 1) * nk, cs] = (
                        jnp.concatenate(gated, axis=0).astype(BF16))

    if nsub > 1:
        expert_acts(1)
    for sb in range(nsub):
        gate(sb)
        if sb + 2 < nsub:
            expert_acts(sb + 2)
    acc_ref[...] += jnp.dot(vt_ref[0], p_ref[...], preferred_element_type=F32)
    act0_ref[...] = jnp.dot(unext_ref[0], xnext_ref[0], preferred_element_type=F32)

    @pl.when(eb == pl.num_programs(1) - 1)
    def _():
        o_ref[...] = h1_ref[...] + acc_ref[...].T


def _peer(h1, xn_t, q_t, keys, u3, vt3):
    n, d = h1.shape
    nt, _, t = xn_t.shape
    nblk, sub, _ = u3.shape
    nsub = PEER_SUBS_PER_STEP
    nhp, nk, dk = keys.shape
    spt = nblk // nsub
    tile = lambda i, e: (i, 0, 0)
    next_sub = lambda i, e: ((e + 1) % spt * nsub, 0, 0)
    next_tile = lambda i, e: (jnp.minimum(i + (e + 1) // spt, nt - 1), 0, 0)
    return pl.pallas_call(
        _peer_kernel,
        grid=(nt, spt),
        in_specs=[pl.BlockSpec((t, d), lambda i, e: (i, 0)),
                  pl.BlockSpec((1, d, t), tile), pl.BlockSpec((1, q_t.shape[1], t), tile),
                  pl.BlockSpec(keys.shape, lambda i, e: (0, 0, 0)),
                  pl.BlockSpec((nsub, sub, d), lambda i, e: (e, 0, 0)),
                  pl.BlockSpec((1, d, nsub * sub), lambda i, e: (e, 0, 0)),
                  pl.BlockSpec((1, sub, d), next_sub), pl.BlockSpec((1, d, t), next_tile)],
        out_specs=pl.BlockSpec((t, d), lambda i, e: (i, 0)),
        out_shape=jax.ShapeDtypeStruct((n, d), F32),
        scratch_shapes=[pltpu.VMEM((nhp, nk, t), F32), pltpu.VMEM((nhp, PEER_TOPK, t), F32),
                        pltpu.VMEM((nhp // 2, nk, t), F32), pltpu.VMEM((nhp // 2, nk, t), F32),
                        pltpu.VMEM((nhp // 2, nk, t), GATE_DTYPE), pltpu.VMEM((nhp // 2, nk, t), GATE_DTYPE),
                        pltpu.VMEM((nsub * sub, t), BF16), pltpu.VMEM((d, t), F32),
                        pltpu.VMEM((sub, t), F32)],
        compiler_params=_cparams("arbitrary", "arbitrary"),
        name="peer",
    )(h1, xn_t, q_t, keys, u3, vt3, u3, xn_t)


def _final_kernel(h_ref, p_ref, gp_ref, wg_ref, wp_ref, gf_ref, y_ref):
    h = h_ref[...]
    gate = jax.nn.sigmoid(jnp.dot(_rms(h, gp_ref[...]).astype(BF16), wg_ref[...], preferred_element_type=F32))
    h = h + gate * jnp.dot(p_ref[...].astype(BF16), wp_ref[...], preferred_element_type=F32)
    y_ref[...] = _rms(h, gf_ref[...])


def _final(h2, p, g_ple, w_gate, w_proj, g_final):
    n, d = h2.shape
    tm = ROW_TILE
    pd = p.shape[1]
    row = lambda i: (i, 0)
    const = lambda i: (0, 0)
    return pl.pallas_call(
        _final_kernel,
        grid=(n // tm,),
        in_specs=[pl.BlockSpec((tm, d), row), pl.BlockSpec((tm, pd), row), pl.BlockSpec((1, d), const),
                  pl.BlockSpec(w_gate.shape, const), pl.BlockSpec(w_proj.shape, const), pl.BlockSpec((1, d), const)],
        out_specs=pl.BlockSpec((tm, d), row),
        out_shape=jax.ShapeDtypeStruct((n, d), F32),
        compiler_params=_cparams("parallel"),
        name="final",
    )(h2, p, g_ple, w_gate, w_proj, g_final)


def _block_diag(w):
    nb, bi, bj = w.shape
    eye = jnp.eye(nb, dtype=w.dtype)
    return (eye[:, None, :, None] * w[:, :, None, :]).reshape(nb * bi, nb * bj)


def _rope_tables(seq):
    half = HEAD_DIM // 2
    inv_freq = ROPE_THETA ** (-jnp.arange(half, dtype=F32) / half)
    ang = jnp.arange(seq, dtype=F32)[:, None] * inv_freq[None, :]
    sin = jnp.sin(ang)
    heads = LANE // HEAD_DIM
    return jnp.tile(jnp.cos(ang), (1, 2 * heads)), jnp.tile(jnp.concatenate([-sin, sin], axis=1), (1, heads))


def _encoder(x, p, wts):
    b, s, d = x.shape
    n = b * s
    aw, lw = wts["aw"], wts["lw"]
    cos, sin = _rope_tables(s)
    x2 = x.reshape(n, d)
    q, k2, v2, xr, gr = _in_proj(x2, wts["mix_g"], wts["w_in"], cos, sin, s, aw, lw)
    attn_n = _attention(q, k2, v2, wts["sink"], wts["attn_g"], s)
    hf = _lru_fwd(xr, wts["conv_w"], wts["conv_b"], wts["wg"][0], wts["bg"][0], wts["lam"][0], b, s)
    lru_n = _lru_bwd(xr, wts["conv_w"], wts["conv_b"], wts["wg"][1], wts["bg"][1], wts["lam"][1],
                     gr, hf, wts["lru_g"], b, s)
    h1, xn_t, q_t = _out_proj(x2, attn_n, lru_n, wts["w_out"], wts["ffn_g"], wts["wq_t"])
    h2 = _peer(h1, xn_t, q_t, wts["keys"], wts["u3"], wts["vt3"])
    y = _final(h2, p.reshape(n, -1), wts["ple_g"], wts["w_gate"], wts["w_proj"], wts["final_g"])
    return y.reshape(b, s, d)


def kernel(x_prompt, x_sample, p_prompt, p_sample, mix_norm_g, w_in, attn_sink, conv_w, conv_b, lru_wa, lru_ba, lru_wx, lru_bx, lru_lambda, attn_out_norm_g, lru_out_norm_g, w_out, ffn_norm_g, peer_wq, peer_keys, peer_u, peer_v, ple_norm_g, ple_w_gate, ple_w_proj, final_norm_g):
    depth = w_in.shape[0]
    assert depth == 1, "single-layer encoder"
    l = 0
    d = w_in.shape[1]
    aw = N_Q_HEADS * HEAD_DIM
    kw = N_KV_HEADS * HEAD_DIM
    lw = conv_w.shape[2]
    assert kw == LANE and aw % LANE == 0
    o1, o2, o3, o4 = aw, aw + kw, aw + 2 * kw, aw + 2 * kw + lw
    w = w_in[l]
    wq_, wk_, wv_, wx_, wg_ = w[:, :o1], w[:, o1:o2], w[:, o2:o3], w[:, o3:o4], w[:, o4:]
    grp = N_Q_HEADS // N_KV_HEADS
    head_order = np.array([m + kv * grp for m in range(grp) for kv in range(N_KV_HEADS)])
    feat_order = (head_order[:, None] * HEAD_DIM + np.arange(HEAD_DIM)[None, :]).reshape(-1)
    wq_ = wq_[:, feat_order]
    w_out_l = jnp.concatenate([w_out[l][:aw][feat_order], w_out[l][aw:]], axis=0)
    w_ext = jnp.concatenate([wq_, wk_, wv_, wx_, wg_], axis=1).astype(BF16)
    nh, _, nk, dk = peer_keys.shape[1:]
    n_exp = peer_u.shape[1]
    assert n_exp == nk * nk and nk == LANE and dk == LANE
    nblk = n_exp // PEER_SUB
    wts = dict(
        aw=aw, lw=lw,
        mix_g=mix_norm_g[l][None], w_in=w_ext, sink=attn_sink[l],
        attn_g=attn_out_norm_g[l][feat_order][None], lru_g=lru_out_norm_g[l][None],
        conv_w=conv_w[l], conv_b=conv_b[l][None],
        wg=[jnp.concatenate([_block_diag(lru_wa[l, k]), _block_diag(lru_wx[l, k])], axis=1).astype(BF16) for k in range(2)],
        bg=[jnp.concatenate([lru_ba[l, k], lru_bx[l, k]])[None] for k in range(2)],
        lam=[lru_lambda[l, k][None] for k in range(2)],
        w_out=w_out_l.astype(BF16), ffn_g=ffn_norm_g[l][None],
        wq_t=peer_wq[l].T.astype(BF16),
        keys=peer_keys[l].reshape(nh * 2, nk, dk).astype(BF16),
        u3=peer_u[l].astype(BF16).reshape(nblk, PEER_SUB, d),
        vt3=peer_v[l].astype(BF16).reshape(nblk // PEER_SUBS_PER_STEP, PEER_SUBS_PER_STEP * PEER_SUB, d).transpose(0, 2, 1),
        ple_g=ple_norm_g[l][None], w_gate=ple_w_gate[l].astype(BF16), w_proj=ple_w_proj[l].astype(BF16),
        final_g=final_norm_g[None],
    )
    y_prompt = _encoder(x_prompt, p_prompt[l], wts)
    y_sample = _encoder(x_sample, p_sample[l], wts)
    return (y_prompt, y_sample)
```

```python
import functools
import math

import jax
import jax.numpy as jnp
import numpy as np
from jax import lax
from jax.experimental import pallas as pl
from jax.experimental.pallas import tpu as pltpu

F32 = jnp.float32
BF16 = jnp.bfloat16

EPS = 1e-6
N_Q_HEADS = 8
N_KV_HEADS = 2
HEAD_DIM = 64
WINDOW = 128
ROPE_THETA = 10000.0
LRU_C = 8.0
CONV_LEFT = 2
PEER_TOPK = 16

LANE = 128
SUBLANE = 8
GATE_DTYPE = jnp.bfloat16
GATE_ROWS = 16
GATE_KEYS = 2
ROW_TILE = 1024
PEER_TOKENS = 512
PEER_SUB = 512
PEER_SUBS_PER_STEP = 4
VMEM_LIMIT = 56 * 1024 * 1024
NEG_INF = float("-inf")


def _cparams(*sem):
    return pltpu.CompilerParams(dimension_semantics=sem, vmem_limit_bytes=VMEM_LIMIT)


def _rms(x, g):
    return x * lax.rsqrt(jnp.mean(x * x, axis=-1, keepdims=True) + EPS) * g


_GELU_K0 = -2.0 * math.sqrt(2.0 / math.pi) * math.log2(math.e)
_GELU_K1 = _GELU_K0 * 0.044715


def _gelu_tanh(x):
    return x / (1.0 + jnp.exp2(x * (_GELU_K0 + _GELU_K1 * (x * x))))


def _in_proj_kernel(x_ref, g_ref, w_ref, cos_ref, sin_ref, q_ref, k_ref, v_ref, xr_ref, gr_ref):
    aw, kw, lw = q_ref.shape[1], LANE, xr_ref.shape[1]
    xn = _rms(x_ref[...], g_ref[...]).astype(BF16)
    z = jnp.dot(xn, w_ref[...], preferred_element_type=F32)
    c = cos_ref[...]
    s = sin_ref[...]
    half = HEAD_DIM // 2
    first = lax.broadcasted_iota(jnp.int32, c.shape, 1) % HEAD_DIM < half

    def rope(t):
        rot = jnp.where(first, pltpu.roll(t, LANE - half, 1), pltpu.roll(t, half, 1))
        return t * c + rot * s

    for m in range(aw // LANE):
        q_ref[:, m * LANE:(m + 1) * LANE] = (rope(z[:, m * LANE:(m + 1) * LANE]) * (HEAD_DIM ** -0.5)).astype(BF16)
    o = aw
    k = rope(z[:, o:o + kw])
    o += kw
    v = z[:, o:o + kw]
    o += kw
    k_ref[...] = k.astype(BF16)
    v_ref[...] = v.astype(BF16)
    xr_ref[...] = z[:, o:o + lw]
    gr_ref[...] = z[:, o + lw:o + 2 * lw]


def _in_proj(x, g, w_ext, cos, sin, seq, aw, lw):
    n, d = x.shape
    tm = ROW_TILE
    per_seq = seq // tm
    row = lambda i: (i, 0)
    const = lambda i: (0, 0)
    pos = lambda i: (i % per_seq, 0)
    return pl.pallas_call(
        _in_proj_kernel,
        grid=(n // tm,),
        in_specs=[pl.BlockSpec((tm, d), row), pl.BlockSpec((1, d), const),
                  pl.BlockSpec(w_ext.shape, const),
                  pl.BlockSpec((tm, LANE), pos), pl.BlockSpec((tm, LANE), pos)],
        out_specs=[pl.BlockSpec((tm, aw), row), pl.BlockSpec((tm, LANE), row),
                   pl.BlockSpec((tm, LANE), row), pl.BlockSpec((tm, lw), row),
                   pl.BlockSpec((tm, lw), row)],
        out_shape=[jax.ShapeDtypeStruct((n, aw), BF16), jax.ShapeDtypeStruct((n, LANE), BF16),
                   jax.ShapeDtypeStruct((n, LANE), BF16), jax.ShapeDtypeStruct((n, lw), F32),
                   jax.ShapeDtypeStruct((n, lw), F32)],
        compiler_params=_cparams("parallel"),
        name="in_proj",
    )(x, g, w_ext, cos, sin)


def _attn_kernel(sink_ref, q_ref, kp_ref, ks_ref, kn_ref, vp_ref, vs_ref, vn_ref, g_ref, o_ref, *, seq):
    qb = q_ref.shape[0]
    blk = WINDOW
    p0 = (pl.program_id(0) % (seq // qb)) * qb
    kf = jnp.concatenate([kp_ref[...], ks_ref[...], kn_ref[...]], axis=0)
    vf = jnp.concatenate([vp_ref[...], vs_ref[...], vn_ref[...]], axis=0)
    lane = lax.broadcasted_iota(jnp.int32, (blk, LANE), 1)
    row = lax.broadcasted_iota(jnp.int32, (blk, 3 * blk), 0)
    col = lax.broadcasted_iota(jnp.int32, (blk, 3 * blk), 1)
    rel = col - row
    band = (rel >= 0) & (rel <= 2 * WINDOW)
    grp = N_Q_HEADS // N_KV_HEADS
    for r in range(qb // blk):
        kpos = p0 + (r - 1) * blk + col
        valid = band & (kpos >= 0) & (kpos < seq)
        q4 = [q_ref[r * blk:(r + 1) * blk, m * LANE:(m + 1) * LANE] for m in range(grp)]
        zero = jnp.zeros_like(q4[0])
        stacked = ([jnp.where(lane < HEAD_DIM, qs, zero) for qs in q4]
                   + [jnp.where(lane >= HEAD_DIM, qs, zero) for qs in q4])
        qs = jnp.concatenate(stacked, axis=0)
        kk = kf[r * blk:(r + 3) * blk]
        vv = vf[r * blk:(r + 3) * blk]
        s_all = lax.dot_general(qs, kk, (((1,), (1,)), ((), ())), preferred_element_type=F32)
        probs = []
        for h in range(N_Q_HEADS):
            s = jnp.where(valid, s_all[h * blk:(h + 1) * blk], -1e30)
            sk = sink_ref[h]
            mx = jnp.maximum(jnp.max(s, axis=1, keepdims=True), sk)
            e = jnp.exp(s - mx)
            den = jnp.sum(e, axis=1, keepdims=True) + jnp.exp(sk - mx)
            probs.append((e * (1.0 / den)).astype(BF16))
        o_all = jnp.dot(jnp.concatenate(probs, axis=0), vv, preferred_element_type=F32)
        attn = jnp.concatenate(
            [jnp.where(lane < HEAD_DIM, o_all[m * blk:(m + 1) * blk], o_all[(grp + m) * blk:(grp + m + 1) * blk])
             for m in range(grp)], axis=1)
        o_ref[r * blk:(r + 1) * blk, :] = _rms(attn, g_ref[...]).astype(BF16)


def _attention(q, k2, v2, sink, g, seq):
    n, aw = q.shape
    qb = ROW_TILE
    per = qb // WINDOW
    nblk = n // WINDOW
    row = lambda i: (i, 0)
    prev = lambda i: (jnp.maximum(i * per - 1, 0), 0)
    nxt = lambda i: (jnp.minimum((i + 1) * per, nblk - 1), 0)
    kw = k2.shape[1]
    return pl.pallas_call(
        functools.partial(_attn_kernel, seq=seq),
        grid=(n // qb,),
        in_specs=[pl.BlockSpec(memory_space=pltpu.SMEM),
                  pl.BlockSpec((qb, aw), row),
                  pl.BlockSpec((WINDOW, kw), prev), pl.BlockSpec((qb, kw), row), pl.BlockSpec((WINDOW, kw), nxt),
                  pl.BlockSpec((WINDOW, kw), prev), pl.BlockSpec((qb, kw), row), pl.BlockSpec((WINDOW, kw), nxt),
                  pl.BlockSpec((1, aw), lambda i: (0, 0))],
        out_specs=pl.BlockSpec((qb, aw), row),
        out_shape=jax.ShapeDtypeStruct((n, aw), BF16),
        compiler_params=_cparams("parallel"),
        name="attention",
    )(sink, q, k2, k2, k2, v2, v2, v2, g)


def _lru_gates(x_ref, xp_ref, xn_ref, cw_ref, cb_ref, wg_ref, bg_ref, lam_ref, ext_ref, a_ref, u_ref,
               has_prev, has_next):
    tb, lw = x_ref.shape
    x = x_ref[...]
    ext_ref[0:SUBLANE, :] = jnp.where(has_prev, xp_ref[...], 0.0)
    ext_ref[SUBLANE:SUBLANE + tb, :] = x
    ext_ref[SUBLANE + tb:2 * SUBLANE + tb, :] = jnp.where(has_next, xn_ref[...], 0.0)
    cw = cw_ref[...]
    xc = cb_ref[...] + cw[2:3] * x
    for j in (0, 1, 3):
        off = SUBLANE + j - CONV_LEFT
        xc = xc + cw[j:j + 1] * ext_ref[off:off + tb, :]
    gates = jnp.dot(xc.astype(BF16), wg_ref[...], preferred_element_type=F32) + bg_ref[...]
    gate_r = jax.nn.sigmoid(gates[:, :lw])
    gate_i = jax.nn.sigmoid(gates[:, lw:])
    lam = lam_ref[...]
    log_sig = jnp.minimum(lam, 0.0) - jnp.log1p(jnp.exp(-jnp.abs(lam)))
    log_a = LRU_C * gate_r * log_sig
    a = jnp.exp(log_a)
    a_ref[...] = a
    u_ref[...] = jnp.sqrt(-jnp.tanh(log_a) * (a * a + 1.0)) * (gate_i * xc)


def _lru_fwd_kernel(x_ref, xp_ref, xn_ref, cw_ref, cb_ref, wg_ref, bg_ref, lam_ref, h_ref,
                    ext_ref, a_ref, u_ref, carry_ref):
    j = pl.program_id(1)
    nt = pl.num_programs(1)
    tb = x_ref.shape[0]

    @pl.when(j == 0)
    def _():
        carry_ref[...] = jnp.zeros_like(carry_ref)

    _lru_gates(x_ref, xp_ref, xn_ref, cw_ref, cb_ref, wg_ref, bg_ref, lam_ref, ext_ref, a_ref, u_ref,
               j > 0, j < nt - 1)

    def body(t, h):
        h = a_ref[pl.ds(t, 1), :] * h + u_ref[pl.ds(t, 1), :]
        h_ref[pl.ds(t, 1), :] = h
        return h

    carry_ref[...] = lax.fori_loop(0, tb, body, carry_ref[...], unroll=8)


def _lru_bwd_kernel(x_ref, xp_ref, xn_ref, cw_ref, cb_ref, wg_ref, bg_ref, lam_ref, gr_ref, hf_ref, g_ref,
                    o_ref, ext_ref, a_ref, u_ref, hb_ref, carry_ref):
    j = pl.program_id(1)
    nt = pl.num_programs(1)
    tb = x_ref.shape[0]

    @pl.when(j == 0)
    def _():
        carry_ref[...] = jnp.zeros_like(carry_ref)

    _lru_gates(x_ref, xp_ref, xn_ref, cw_ref, cb_ref, wg_ref, bg_ref, lam_ref, ext_ref, a_ref, u_ref,
               j < nt - 1, j > 0)

    def body(i, h):
        t = tb - 1 - i
        h = a_ref[pl.ds(t, 1), :] * h + u_ref[pl.ds(t, 1), :]
        hb_ref[pl.ds(t, 1), :] = h
        return h

    carry_ref[...] = lax.fori_loop(0, tb, body, carry_ref[...], unroll=8)
    lru = _gelu_tanh(gr_ref[...]) * (hf_ref[...] + hb_ref[...])
    o_ref[...] = _rms(lru, g_ref[...]).astype(BF16)


def _lru_specs(batch, seq, lw, tb, reverse):
    nt = seq // tb
    per8 = tb // SUBLANE
    n8 = batch * seq // SUBLANE

    def blk(b, j):
        return b * nt + (nt - 1 - j if reverse else j)

    main = lambda b, j: (blk(b, j), 0)
    prev = lambda b, j: (jnp.maximum(blk(b, j) * per8 - 1, 0), 0)
    nxt = lambda b, j: (jnp.minimum((blk(b, j) + 1) * per8, n8 - 1), 0)
    const = lambda b, j: (0, 0)
    specs = [pl.BlockSpec((tb, lw), main), pl.BlockSpec((SUBLANE, lw), prev), pl.BlockSpec((SUBLANE, lw), nxt),
             pl.BlockSpec((4, lw), const), pl.BlockSpec((1, lw), const),
             pl.BlockSpec((lw, 2 * lw), const), pl.BlockSpec((1, 2 * lw), const), pl.BlockSpec((1, lw), const)]
    return specs, main, const, nt


def _lru_fwd(xr, cw, cb, wg, bg, lam, batch, seq):
    n, lw = xr.shape
    tb = ROW_TILE
    specs, main, _, nt = _lru_specs(batch, seq, lw, tb, reverse=False)
    return pl.pallas_call(
        _lru_fwd_kernel,
        grid=(batch, nt),
        in_specs=specs,
        out_specs=pl.BlockSpec((tb, lw), main),
        out_shape=jax.ShapeDtypeStruct((n, lw), F32),
        scratch_shapes=[pltpu.VMEM((tb + 2 * SUBLANE, lw), F32), pltpu.VMEM((tb, lw), F32),
                        pltpu.VMEM((tb, lw), F32), pltpu.VMEM((1, lw), F32)],
        compiler_params=_cparams("parallel", "arbitrary"),
        name="lru_fwd",
    )(xr, xr, xr, cw, cb, wg, bg, lam)


def _lru_bwd(xr, cw, cb, wg, bg, lam, gr, hf, g, batch, seq):
    n, lw = xr.shape
    tb = ROW_TILE
    specs, main, const, nt = _lru_specs(batch, seq, lw, tb, reverse=True)
    specs = specs + [pl.BlockSpec((tb, lw), main), pl.BlockSpec((tb, lw), main), pl.BlockSpec((1, lw), const)]
    return pl.pallas_call(
        _lru_bwd_kernel,
        grid=(batch, nt),
        in_specs=specs,
        out_specs=pl.BlockSpec((tb, lw), main),
        out_shape=jax.ShapeDtypeStruct((n, lw), BF16),
        scratch_shapes=[pltpu.VMEM((tb + 2 * SUBLANE, lw), F32), pltpu.VMEM((tb, lw), F32),
                        pltpu.VMEM((tb, lw), F32), pltpu.VMEM((tb, lw), F32), pltpu.VMEM((1, lw), F32)],
        compiler_params=_cparams("parallel", "arbitrary"),
        name="lru_bwd",
    )(xr, xr, xr, cw, cb, wg, bg, lam, gr, hf, g)


def _out_proj_kernel(x_ref, at_ref, lr_ref, wo_ref, g_ref, wqt_ref, h1_ref, xnt_ref, qt_ref):
    aw = at_ref.shape[1]
    h1 = (x_ref[...] + jnp.dot(at_ref[...], wo_ref[0:aw, :], preferred_element_type=F32)
          + jnp.dot(lr_ref[...], wo_ref[aw:, :], preferred_element_type=F32))
    h1_ref[...] = h1
    xnt = _rms(h1, g_ref[...]).T.astype(BF16)
    xnt_ref[0] = xnt
    qt_ref[0] = jnp.dot(wqt_ref[...], xnt, preferred_element_type=F32).astype(BF16)


def _out_proj(x, attn_n, lru_n, w_out, g, wq_t):
    n, d = x.shape
    tm = PEER_TOKENS
    aw, lw = attn_n.shape[1], lru_n.shape[1]
    qd = wq_t.shape[0]
    row = lambda i: (i, 0)
    const = lambda i: (0, 0)
    tile = lambda i: (i, 0, 0)
    return pl.pallas_call(
        _out_proj_kernel,
        grid=(n // tm,),
        in_specs=[pl.BlockSpec((tm, d), row), pl.BlockSpec((tm, aw), row), pl.BlockSpec((tm, lw), row),
                  pl.BlockSpec(w_out.shape, const), pl.BlockSpec((1, d), const), pl.BlockSpec(wq_t.shape, const)],
        out_specs=[pl.BlockSpec((tm, d), row), pl.BlockSpec((1, d, tm), tile), pl.BlockSpec((1, qd, tm), tile)],
        out_shape=[jax.ShapeDtypeStruct((n, d), F32), jax.ShapeDtypeStruct((n // tm, d, tm), BF16),
                   jax.ShapeDtypeStruct((n // tm, qd, tm), BF16)],
        compiler_params=_cparams("parallel"),
        name="out_proj",
    )(x, attn_n, lru_n, w_out, g, wq_t)


def _batcher_network(lo, hi):
    def merge(lo, hi, r):
        step = r * 2
        if step < hi - lo:
            yield from merge(lo, hi, step)
            yield from merge(lo + r, hi, step)
            yield from [(i, i + r) for i in range(lo + r, hi - r, step)]
        else:
            yield (lo, lo + r)

    if hi - lo >= 1:
        mid = lo + (hi - lo) // 2
        yield from _batcher_network(lo, mid)
        yield from _batcher_network(mid + 1, hi)
        yield from merge(lo, hi, 1)


def _exchange(x, i, j):
    x[i], x[j] = jnp.maximum(x[i], x[j]), jnp.minimum(x[i], x[j])


def _sorted_top16(groups):
    n = PEER_TOPK
    assert len(groups) == n
    x = list(groups)
    for i, j in _batcher_network(0, n - 1):
        _exchange(x, i, j)
    shift = SUBLANE // 2
    while shift:
        rolled = [pltpu.roll(v, shift, 0) for v in x]
        x = [jnp.maximum(x[k], rolled[n - 1 - k]) for k in range(n)]
        d = n // 2
        while d:
            for k in range(n):
                if not k & d:
                    _exchange(x, k, k + d)
            d //= 2
        shift //= 2
    return x


def _prefix_count(test, rows):
    w = jnp.where
    g1 = test(rows[7])
    g2 = test(w(g1, rows[11], rows[3]))
    g3 = test(w(g1, w(g2, rows[13], rows[9]), w(g2, rows[5], rows[1])))
    g4 = test(w(g1, w(g2, w(g3, rows[14], rows[12]), w(g3, rows[10], rows[8])),
                w(g2, w(g3, rows[6], rows[4]), w(g3, rows[2], rows[0]))))
    cnt = w(g1, 8.0, 0.0) + w(g2, 4.0, 0.0) + w(g3, 2.0, 0.0) + w(g4, 1.0, 0.0)
    return w(test(rows[15]), 16.0, cnt)


def _peer_select_head(h, slot, qt_ref, keys_ref, s_ref, a_ref, e1_ref, ell_ref, rank_ref, e2_ref):
    t = qt_ref.shape[2]
    nch = t // LANE
    nk, dk = keys_ref.shape[1:]
    ngrp = nk // SUBLANE

    for half in range(2):
        hp = 2 * h + half
        q = qt_ref[0, pl.ds(pl.multiple_of(hp * dk, dk), dk), :]
        s_ref[half] = jnp.dot(keys_ref[hp], q, preferred_element_type=F32)

    for half in range(2):
        for c in range(nch):
            cs = slice(c * LANE, (c + 1) * LANE)
            s0 = s_ref[half, :, cs]
            groups = [s0[g * SUBLANE:(g + 1) * SUBLANE] for g in range(ngrp)]
            top = _sorted_top16(groups)
            for k in range(PEER_TOPK):
                a_ref[half, k:k + 1, cs] = top[k][0:1]
            e = jnp.exp(s0 - top[0][0:1])
            if half == 0:
                e1_ref[slot, h, :, cs] = e
            else:
                e2_ref[slot, h, :, cs] = e.astype(GATE_DTYPE)
                rank = [_prefix_count(lambda thr, v=v: thr > v, top) for v in groups]
                rank_ref[slot, h, :, cs] = jnp.concatenate(rank, axis=0).astype(GATE_DTYPE)

    rows8 = lax.broadcasted_iota(jnp.int32, (SUBLANE, LANE), 0)

    if True:
        for c in range(nch):
            cs = slice(c * LANE, (c + 1) * LANE)
            a = a_ref[0, :, cs]
            b = a_ref[1, :, cs]
            a_lo, b_lo = a[0:SUBLANE], b[0:SUBLANE]
            blocks = [a[0:1] + b_lo, a[0:1] + b[SUBLANE:], a[SUBLANE:] + b[0:1]]
            blocks += [a[i:i + 1] + b_lo for i in (1, 2, 3)]
            blocks += [jnp.where(rows8 >= 4, a_lo + b[j:j + 1], NEG_INF) for j in (0, 1, 2)]
            top = a[0:1] + b[0:1]
            z = jnp.zeros((1, LANE), F32)
            m = top
            for k in range(PEER_TOPK):
                m = blocks[0]
                for blk in blocks[1:]:
                    m = jnp.maximum(m, blk)
                m = jnp.max(m, axis=0, keepdims=True)
                z = z + jnp.exp(m - top)
                if k + 1 < PEER_TOPK:
                    blocks = [jnp.where(blk == m, NEG_INF, blk) for blk in blocks]
            tau = jnp.broadcast_to(m, (SUBLANE, LANE))
            e1_ref[slot, h, :, cs] = e1_ref[slot, h, :, cs] * (1.0 / z)
            b_rows = [jnp.broadcast_to(b[r:r + 1], (SUBLANE, LANE)) for r in range(PEER_TOPK)]
            s1 = s_ref[0, :, cs]
            ell = [_prefix_count(lambda thr, v=s1[g * SUBLANE:(g + 1) * SUBLANE]: v + thr >= tau, b_rows)
                   for g in range(ngrp)]
            ell_ref[slot, h, :, cs] = jnp.concatenate(ell, axis=0)


def _peer_kernel(h1_ref, xnt_ref, qt_ref, keys_ref, u_ref, vt_ref, o_ref,
                 s_ref, a_ref, e1_ref, ell_ref, rank_ref, e2_ref, p_ref, acc_ref):
    row = pl.program_id(0)
    eb = pl.program_id(1)
    build, use = row % 2, (row + 1) % 2
    nsub = u_ref.shape[0]
    sub = u_ref.shape[1]
    nk = keys_ref.shape[1]
    nh = keys_ref.shape[0] // 2
    t = xnt_ref.shape[2]
    nch = t // LANE
    per_sub = sub // nk
    assert SUBLANE % per_sub == 0 and nsub % (SUBLANE // per_sub) == 0
    ngrp = nk // GATE_ROWS

    @pl.when((row == 0) & (eb == 0))
    def _():
        for ref in (e1_ref, ell_ref, rank_ref, e2_ref):
            ref[1] = jnp.zeros(ref.shape[1:], ref.dtype)

    @pl.when(eb == 0)
    def _():
        acc_ref[...] = jnp.zeros_like(acc_ref)

    _peer_select_head(eb, build, qt_ref, keys_ref, s_ref, a_ref, e1_ref, ell_ref, rank_ref, e2_ref)

    xnt = xnt_ref[0]
    acts = {}

    def expert_acts(sb):
        acts[sb] = jnp.dot(u_ref[sb], xnt, preferred_element_type=F32)

    def gate(sb):
        act = acts.pop(sb)
        i0 = pl.multiple_of((eb * nsub + sb) * per_sub // SUBLANE * SUBLANE, SUBLANE)
        r0 = sb * per_sub % SUBLANE
        for c in range(nch):
            cs = slice(c * LANE, (c + 1) * LANE)
            ell8 = [ell_ref[use, h, pl.ds(i0, SUBLANE), cs] for h in range(nh)]
            e18 = [e1_ref[use, h, pl.ds(i0, SUBLANE), cs] for h in range(nh)]
            for ii0 in range(0, per_sub, GATE_KEYS):
                keys = range(ii0, ii0 + GATE_KEYS)
                w = {ii: [None] * ngrp for ii in keys}
                for h in range(nh):
                    rows = {ii: (jnp.broadcast_to(ell8[h][r0 + ii:r0 + ii + 1], (GATE_ROWS, LANE)).astype(GATE_DTYPE),
                                 jnp.broadcast_to(e18[h][r0 + ii:r0 + ii + 1], (GATE_ROWS, LANE)).astype(GATE_DTYPE))
                            for ii in keys}
                    for g in range(ngrp):
                        gs = slice(g * GATE_ROWS, (g + 1) * GATE_ROWS)
                        rank = rank_ref[use, h, gs, cs]
                        e2 = e2_ref[use, h, gs, cs]
                        for ii in keys:
                            ell, e1 = rows[ii]
                            term = jnp.where(rank < ell, e2 * e1, jnp.zeros_like(e1))
                            w[ii][g] = term if h == 0 else w[ii][g] + term
                for ii in keys:
                    ga = _gelu_tanh(act[ii * nk:(ii + 1) * nk, cs]).astype(GATE_DTYPE)
                    gated = [w[ii][g] * ga[g * GATE_ROWS:(g + 1) * GATE_ROWS] for g in range(ngrp)]
                    p_ref[sb * sub + ii * nk:sb * sub + (ii + 1) * nk, cs] = (
                        jnp.concatenate(gated, axis=0).astype(BF16))

    expert_acts(0)
    if nsub > 1:
        expert_acts(1)
    for sb in range(nsub):
        gate(sb)
        if sb + 2 < nsub:
            expert_acts(sb + 2)
    acc_ref[...] += jnp.dot(vt_ref[0], p_ref[...], preferred_element_type=F32)

    @pl.when(eb == pl.num_programs(1) - 1)
    def _():
        o_ref[...] = h1_ref[...] + acc_ref[...].T


def _peer(h1, xn_t, q_t, keys, u3, vt3):
    n, d = h1.shape
    nt, _, t = xn_t.shape
    nblk, sub, _ = u3.shape
    nsub = PEER_SUBS_PER_STEP
    nhp, nk, dk = keys.shape
    nh = nhp // 2
    assert nblk // nsub == nh, "one head's routing tables are built per expert block"
    dense = lambda i: jnp.maximum(i - 1, 0)
    built = lambda i: jnp.minimum(i, nt - 1)
    return pl.pallas_call(
        _peer_kernel,
        grid=(nt + 1, nh),
        in_specs=[pl.BlockSpec((t, d), lambda i, e: (dense(i), 0)),
                  pl.BlockSpec((1, d, t), lambda i, e: (dense(i), 0, 0)),
                  pl.BlockSpec((1, q_t.shape[1], t), lambda i, e: (built(i), 0, 0)),
                  pl.BlockSpec(keys.shape, lambda i, e: (0, 0, 0)),
                  pl.BlockSpec((nsub, sub, d), lambda i, e: (e, 0, 0)),
                  pl.BlockSpec((1, d, nsub * sub), lambda i, e: (e, 0, 0))],
        out_specs=pl.BlockSpec((t, d), lambda i, e: (dense(i), 0)),
        out_shape=jax.ShapeDtypeStruct((n, d), F32),
        scratch_shapes=[pltpu.VMEM((2, nk, t), F32), pltpu.VMEM((2, PEER_TOPK, t), F32),
                        pltpu.VMEM((2, nh, nk, t), F32), pltpu.VMEM((2, nh, nk, t), F32),
                        pltpu.VMEM((2, nh, nk, t), GATE_DTYPE), pltpu.VMEM((2, nh, nk, t), GATE_DTYPE),
                        pltpu.VMEM((nsub * sub, t), BF16), pltpu.VMEM((d, t), F32)],
        compiler_params=_cparams("arbitrary", "arbitrary"),
        name="peer",
    )(h1, xn_t, q_t, keys, u3, vt3)


def _final_kernel(h_ref, p_ref, gp_ref, wg_ref, wp_ref, gf_ref, y_ref):
    h = h_ref[...]
    gate = jax.nn.sigmoid(jnp.dot(_rms(h, gp_ref[...]).astype(BF16), wg_ref[...], preferred_element_type=F32))
    h = h + gate * jnp.dot(p_ref[...].astype(BF16), wp_ref[...], preferred_element_type=F32)
    y_ref[...] = _rms(h, gf_ref[...])


def _final(h2, p, g_ple, w_gate, w_proj, g_final):
    n, d = h2.shape
    tm = ROW_TILE
    pd = p.shape[1]
    row = lambda i: (i, 0)
    const = lambda i: (0, 0)
    return pl.pallas_call(
        _final_kernel,
        grid=(n // tm,),
        in_specs=[pl.BlockSpec((tm, d), row), pl.BlockSpec((tm, pd), row), pl.BlockSpec((1, d), const),
                  pl.BlockSpec(w_gate.shape, const), pl.BlockSpec(w_proj.shape, const), pl.BlockSpec((1, d), const)],
        out_specs=pl.BlockSpec((tm, d), row),
        out_shape=jax.ShapeDtypeStruct((n, d), F32),
        compiler_params=_cparams("parallel"),
        name="final",
    )(h2, p, g_ple, w_gate, w_proj, g_final)


def _block_diag(w):
    nb, bi, bj = w.shape
    eye = jnp.eye(nb, dtype=w.dtype)
    return (eye[:, None, :, None] * w[:, :, None, :]).reshape(nb * bi, nb * bj)


def _rope_tables(seq):
    half = HEAD_DIM // 2
    inv_freq = ROPE_THETA ** (-jnp.arange(half, dtype=F32) / half)
    ang = jnp.arange(seq, dtype=F32)[:, None] * inv_freq[None, :]
    sin = jnp.sin(ang)
    heads = LANE // HEAD_DIM
    return jnp.tile(jnp.cos(ang), (1, 2 * heads)), jnp.tile(jnp.concatenate([-sin, sin], axis=1), (1, heads))


def _encoder(x, p, wts):
    b, s, d = x.shape
    n = b * s
    aw, lw = wts["aw"], wts["lw"]
    cos, sin = _rope_tables(s)
    x2 = x.reshape(n, d)
    q, k2, v2, xr, gr = _in_proj(x2, wts["mix_g"], wts["w_in"], cos, sin, s, aw, lw)
    attn_n = _attention(q, k2, v2, wts["sink"], wts["attn_g"], s)
    hf = _lru_fwd(xr, wts["conv_w"], wts["conv_b"], wts["wg"][0], wts["bg"][0], wts["lam"][0], b, s)
    lru_n = _lru_bwd(xr, wts["conv_w"], wts["conv_b"], wts["wg"][1], wts["bg"][1], wts["lam"][1],
                     gr, hf, wts["lru_g"], b, s)
    h1, xn_t, q_t = _out_proj(x2, attn_n, lru_n, wts["w_out"], wts["ffn_g"], wts["wq_t"])
    h2 = _peer(h1, xn_t, q_t, wts["keys"], wts["u3"], wts["vt3"])
    y = _final(h2, p.reshape(n, -1), wts["ple_g"], wts["w_gate"], wts["w_proj"], wts["final_g"])
    return y.reshape(b, s, d)


def kernel(x_prompt, x_sample, p_prompt, p_sample, mix_norm_g, w_in, attn_sink, conv_w, conv_b, lru_wa, lru_ba, lru_wx, lru_bx, lru_lambda, attn_out_norm_g, lru_out_norm_g, w_out, ffn_norm_g, peer_wq, peer_keys, peer_u, peer_v, ple_norm_g, ple_w_gate, ple_w_proj, final_norm_g):
    depth = w_in.shape[0]
    assert depth == 1, "single-layer encoder"
    l = 0
    d = w_in.shape[1]
    aw = N_Q_HEADS * HEAD_DIM
    kw = N_KV_HEADS * HEAD_DIM
    lw = conv_w.shape[2]
    assert kw == LANE and aw % LANE == 0
    o1, o2, o3, o4 = aw, aw + kw, aw + 2 * kw, aw + 2 * kw + lw
    w = w_in[l]
    wq_, wk_, wv_, wx_, wg_ = w[:, :o1], w[:, o1:o2], w[:, o2:o3], w[:, o3:o4], w[:, o4:]
    grp = N_Q_HEADS // N_KV_HEADS
    head_order = np.array([m + kv * grp for m in range(grp) for kv in range(N_KV_HEADS)])
    feat_order = (head_order[:, None] * HEAD_DIM + np.arange(HEAD_DIM)[None, :]).reshape(-1)
    wq_ = wq_[:, feat_order]
    w_out_l = jnp.concatenate([w_out[l][:aw][feat_order], w_out[l][aw:]], axis=0)
    w_ext = jnp.concatenate([wq_, wk_, wv_, wx_, wg_], axis=1).astype(BF16)
    nh, _, nk, dk = peer_keys.shape[1:]
    n_exp = peer_u.shape[1]
    assert n_exp == nk * nk and nk == LANE and dk == LANE
    nblk = n_exp // PEER_SUB
    wts = dict(
        aw=aw, lw=lw,
        mix_g=mix_norm_g[l][None], w_in=w_ext, sink=attn_sink[l],
        attn_g=attn_out_norm_g[l][feat_order][None], lru_g=lru_out_norm_g[l][None],
        conv_w=conv_w[l], conv_b=conv_b[l][None],
        wg=[jnp.concatenate([_block_diag(lru_wa[l, k]), _block_diag(lru_wx[l, k])], axis=1).astype(BF16) for k in range(2)],
        bg=[jnp.concatenate([lru_ba[l, k], lru_bx[l, k]])[None] for k in range(2)],
        lam=[lru_lambda[l, k][None] for k in range(2)],
        w_out=w_out_l.astype(BF16), ffn_g=ffn_norm_g[l][None],
        wq_t=peer_wq[l].T.astype(BF16),
        keys=peer_keys[l].reshape(nh * 2, nk, dk).astype(BF16),
        u3=peer_u[l].astype(BF16).reshape(nblk, PEER_SUB, d),
        vt3=peer_v[l].astype(BF16).reshape(nblk // PEER_SUBS_PER_STEP, PEER_SUBS_PER_STEP * PEER_SUB, d).transpose(0, 2, 1),
        ple_g=ple_norm_g[l][None], w_gate=ple_w_gate[l].astype(BF16), w_proj=ple_w_proj[l].astype(BF16),
        final_g=final_norm_g[None],
    )
    y_prompt = _encoder(x_prompt, p_prompt[l], wts)
    y_sample = _encoder(x_sample, p_sample[l], wts)
    return (y_prompt, y_sample)
```
